```python
import jax, jax.numpy as jnp
from jax import lax
import numpy as np

D_MODEL = 1024
BATCH = 4
SEQ = 4096
DEPTH = 4

CHUNK = 64
N_META = 16
D_CONV = D_MODEL // 2
N_SB_HEADS = 8
SB_HEAD_DIM = 64
D_SB = N_SB_HEADS * SB_HEAD_DIM
D_MIX = D_CONV + D_SB
CONV_WIDTH = 31
Q_BLOCK = 128
D_IN_PROJ = 3 * D_CONV + 4 * D_SB
RMS_EPS = 1e-6
LN_EPS = 1e-5

kernel_name = "hymba_conformer_stickbreaking_trunk"


def rms_norm(x, g):
    xf = x.astype(jnp.float32)
    y = xf * lax.rsqrt(jnp.mean(xf * xf, axis=-1, keepdims=True) + RMS_EPS)
    return (y * g.astype(jnp.float32)).astype(x.dtype)


def layer_norm(x, g, b):
    xf = x.astype(jnp.float32)
    mu = jnp.mean(xf, axis=-1, keepdims=True)
    var = jnp.mean(jnp.square(xf - mu), axis=-1, keepdims=True)
    y = (xf - mu) * lax.rsqrt(var + LN_EPS)
    return (y * g.astype(jnp.float32) + b.astype(jnp.float32)).astype(x.dtype)


def causal_depthwise_conv(x, w, b):
    c = x.shape[-1]
    y = lax.conv_general_dilated(
        x, w.astype(x.dtype)[:, None, :],
        window_strides=(1,), padding=[(CONV_WIDTH - 1, 0)],
        dimension_numbers=("NWC", "WIO", "NWC"), feature_group_count=c)
    return y + b.astype(x.dtype)


def stick_breaking_attention(q, k, v):
    bsz, seq_len, n_heads, head_dim = q.shape
    lp = -(-seq_len // Q_BLOCK) * Q_BLOCK
    pad = ((0, 0), (0, lp - seq_len), (0, 0), (0, 0))
    q, k, v = (jnp.pad(t, pad).transpose(0, 2, 1, 3) for t in (q, k, v))
    n_blocks = lp // Q_BLOCK
    q_blocks = q.reshape(bsz, n_heads, n_blocks, Q_BLOCK, head_dim).transpose(2, 0, 1, 3, 4)
    key_pos = jnp.arange(lp)
    scale = head_dim ** -0.5

    def one_block(args):
        q_blk, blk = args
        z = jnp.einsum("bhqd,bhkd->bhqk", q_blk, k).astype(jnp.float32) * scale
        q_pos = blk * Q_BLOCK + jnp.arange(Q_BLOCK)
        visible = key_pos[None, :] < q_pos[:, None]
        log_stay = jnp.where(visible, jax.nn.log_sigmoid(-z), 0.0)
        tail = lax.cumsum(log_stay, axis=3, reverse=True) - log_stay
        a = jnp.where(visible, jnp.exp(jax.nn.log_sigmoid(z) + tail), 0.0)
        return jnp.einsum("bhqk,bhkd->bhqd", a.astype(v.dtype), v)

    out = lax.map(one_block, (q_blocks, jnp.arange(n_blocks)))
    out = out.transpose(1, 0, 3, 2, 4).reshape(bsz, lp, n_heads, head_dim)
    return out[:, :seq_len]


def hybrid_layer(h, pre_g, post_g, w_in, conv_w, conv_b, conv_ln_g, conv_ln_b, w_pw2, b_pw2, w_out):
    bsz, seq_len, _ = h.shape
    u = rms_norm(h, pre_g)
    proj = u @ w_in.astype(u.dtype)
    glu_a, glu_b, conv_gate, q, k, v, sb_gate = jnp.split(
        proj, np.cumsum([D_CONV, D_CONV, D_CONV, D_SB, D_SB, D_SB]).tolist(), axis=-1)

    c = glu_a * jax.nn.sigmoid(glu_b)
    c = causal_depthwise_conv(c, conv_w, conv_b)
    c = jax.nn.silu(layer_norm(c, conv_ln_g, conv_ln_b))
    c = c @ w_pw2.astype(c.dtype) + b_pw2.astype(c.dtype)
    c = c * jax.nn.silu(conv_gate)

    heads = lambda t: t.reshape(bsz, seq_len, N_SB_HEADS, SB_HEAD_DIM)
    s = stick_breaking_attention(heads(q), heads(k), heads(v)).reshape(bsz, seq_len, D_SB)
    s = s * jax.nn.silu(sb_gate)

    mixed = jnp.concatenate([c, s], axis=-1) @ w_out.astype(h.dtype)
    return h + rms_norm(mixed, post_g)


def setup_inputs(seed: int = 0) -> dict:
    key = jax.random.key(seed)
    ks = jax.random.split(key, 12)
    f32 = jnp.float32
    nrm = lambda k, shape, s: jax.random.normal(k, shape, f32) * s
    return {
        "x": nrm(ks[0], (BATCH, SEQ, D_MODEL), 1.0),
        "meta_tokens": nrm(ks[1], (N_META, D_MODEL), 1.0),
        "pre_norm_g": 1.0 + nrm(ks[2], (DEPTH, D_MODEL), 0.02),
        "post_norm_g": 1.0 + nrm(ks[3], (DEPTH, D_MODEL), 0.02),
        "w_in": nrm(ks[4], (DEPTH, D_MODEL, D_IN_PROJ), D_MODEL ** -0.5),
        "conv_w": nrm(ks[5], (DEPTH, CONV_WIDTH, D_CONV), CONV_WIDTH ** -0.5),
        "conv_b": nrm(ks[6], (DEPTH, D_CONV), 0.02),
        "conv_ln_g": 1.0 + nrm(ks[7], (DEPTH, D_CONV), 0.02),
        "conv_ln_b": nrm(ks[8], (DEPTH, D_CONV), 0.02),
        "w_pw2": nrm(ks[9], (DEPTH, D_CONV, D_CONV), D_CONV ** -0.5),
        "b_pw2": nrm(ks[10], (DEPTH, D_CONV), 0.02),
        "w_out": nrm(ks[11], (DEPTH, D_MIX, D_MODEL), D_MIX ** -0.5),
    }


def reference(x, meta_tokens, pre_norm_g, post_norm_g, w_in, conv_w, conv_b, conv_ln_g,
              conv_ln_b, w_pw2, b_pw2, w_out):
    bsz = x.shape[0]
    meta = jnp.broadcast_to(meta_tokens.astype(x.dtype)[None], (bsz, N_META, D_MODEL))
    h = jnp.concatenate([meta, x], axis=1)
    for l in range(DEPTH):
        h = hybrid_layer(h, pre_norm_g[l], post_norm_g[l], w_in[l], conv_w[l], conv_b[l],
                         conv_ln_g[l], conv_ln_b[l], w_pw2[l], b_pw2[l], w_out[l])
    return h[:, N_META:]
```

```python
import functools

import jax
import jax.numpy as jnp
from jax import lax
from jax.experimental import pallas as pl
from jax.experimental.pallas import tpu as pltpu

N_META = 16
N_HEADS = 8
HEAD_DIM = 64
CONV_WIDTH = 31
RMS_EPS = 1e-6
LN_EPS = 1e-5

LANES = 128
ROW_TILE = 384
ATT_BLOCK = 128
FIRST_STEP_BLOCKS = 3
CONV_HALO = 32
CONV_CHUNK = 32
TRI_ROWS = ATT_BLOCK + 16
LOG_F32_UNDERFLOW = -104.0
VMEM_LIMIT = 48 * 1024 * 1024

_NT = (((1,), (1,)), ((), ()))


def _sigmoid(x):
    return 1.0 / (1.0 + jnp.exp(-x))


def _dot(a, b):
    return jnp.dot(a, b, preferred_element_type=jnp.float32)


def _in_proj_kernel(h_ref, g_ref, wn_ref, wt_ref,
                    glu_ref, cgate_ref, k_ref, sgate_ref, qt_ref, vt_ref, *, d_grp, n_sub):
    x = h_ref[...]
    ms = jnp.mean(x * x, axis=-1, keepdims=True)
    u = (x * lax.rsqrt(ms + RMS_EPS) * g_ref[...]).astype(jnp.bfloat16)

    a = _dot(u, wn_ref[:, 0 * d_grp:1 * d_grp])
    b = _dot(u, wn_ref[:, 1 * d_grp:2 * d_grp])
    glu_ref[...] = a * _sigmoid(b)
    cg = _dot(u, wn_ref[:, 2 * d_grp:3 * d_grp])
    cgate_ref[...] = cg * _sigmoid(cg)
    k_ref[...] = _dot(u, wn_ref[:, 3 * d_grp:4 * d_grp]).astype(jnp.bfloat16)
    sg = _dot(u, wn_ref[:, 4 * d_grp:5 * d_grp])
    sgate_ref[...] = sg * _sigmoid(sg)

    qt = lax.dot_general(wt_ref[0:d_grp, :], u, _NT, preferred_element_type=jnp.float32)
    qt = (qt * (HEAD_DIM ** -0.5)).astype(jnp.bfloat16)
    vt = lax.dot_general(wt_ref[d_grp:2 * d_grp, :], u, _NT,
                         preferred_element_type=jnp.float32).astype(jnp.bfloat16)
    for c in range(n_sub):
        qt_ref[c] = qt[:, c * ATT_BLOCK:(c + 1) * ATT_BLOCK]
        vt_ref[c] = vt[:, c * ATT_BLOCK:(c + 1) * ATT_BLOCK]


def _in_proj(h, pre_g, w_nat, w_t, layer):
    bsz, lp, d = h.shape
    d_grp = w_t.shape[1] // 2
    n_tiles = lp // ROW_TILE
    n_sub = ROW_TILE // ATT_BLOCK
    n_blk = lp // ATT_BLOCK
    row_f32 = jax.ShapeDtypeStruct((bsz, lp, d_grp), jnp.float32)
    row_bf16 = jax.ShapeDtypeStruct((bsz, lp, d_grp), jnp.bfloat16)
    fm_bf16 = jax.ShapeDtypeStruct((bsz, n_blk, d_grp, ATT_BLOCK), jnp.bfloat16)
    row_spec = pl.BlockSpec((None, ROW_TILE, d_grp), lambda b, i: (b, i, 0))
    fm_spec = pl.BlockSpec((None, n_sub, d_grp, ATT_BLOCK), lambda b, i: (b, i, 0, 0))
    return pl.pallas_call(
        functools.partial(_in_proj_kernel, d_grp=d_grp, n_sub=n_sub),
        grid=(bsz, n_tiles),
        in_specs=[
            pl.BlockSpec((None, ROW_TILE, d), lambda b, i: (b, i, 0)),
            pl.BlockSpec((None, 1, d), lambda b, i: (layer, 0, 0)),
            pl.BlockSpec((None,) + w_nat.shape[1:], lambda b, i: (layer, 0, 0)),
            pl.BlockSpec((None,) + w_t.shape[1:], lambda b, i: (layer, 0, 0)),
        ],
        out_specs=[row_spec, row_spec, row_spec, row_spec, fm_spec, fm_spec],
        out_shape=[row_f32, row_f32, row_bf16, row_f32, fm_bf16, fm_bf16],
        compiler_params=pltpu.CompilerParams(
            dimension_semantics=("parallel", "parallel"), vmem_limit_bytes=VMEM_LIMIT),
        name="in_proj",
    )(h, pre_g, w_nat, w_t)


def _conv_mix_kernel(cur_ref, halo_ref, cgate_ref, cw_ref, cb_ref, lg_ref, lb_ref, wp_ref, bp_ref,
                     c_ref, xs_ref, y_ref):
    i = pl.program_id(1)
    halo = halo_ref[...]
    xs_ref[0:CONV_HALO, :] = jnp.where(i > 0, halo, jnp.zeros_like(halo))
    xs_ref[CONV_HALO:, :] = cur_ref[...]

    cb = cb_ref[...]
    lg = lg_ref[...]
    lb = lb_ref[...]
    first = CONV_HALO - (CONV_WIDTH - 1)
    for r0 in range(0, ROW_TILE, CONV_CHUNK):
        acc = jnp.broadcast_to(cb, (CONV_CHUNK, cb.shape[-1]))
        for j in range(CONV_WIDTH):
            acc = acc + cw_ref[j:j + 1, :] * xs_ref[r0 + first + j:r0 + first + j + CONV_CHUNK, :]
        mu = jnp.mean(acc, axis=-1, keepdims=True)
        cen = acc - mu
        var = jnp.mean(cen * cen, axis=-1, keepdims=True)
        y = cen * lax.rsqrt(var + LN_EPS) * lg + lb
        y_ref[r0:r0 + CONV_CHUNK, :] = (y * _sigmoid(y)).astype(jnp.bfloat16)

    p = _dot(y_ref[...], wp_ref[...]) + bp_ref[...]
    c_ref[...] = (p * cgate_ref[...]).astype(jnp.bfloat16)


def _conv_mix(glu, cgate, conv_w, conv_b, ln_g, ln_b, w_pw2, b_pw2, layer):
    bsz, lp, dc = glu.shape
    n_tiles = lp // ROW_TILE
    halo_per_tile = ROW_TILE // CONV_HALO
    vec_spec = pl.BlockSpec((None, 1, dc), lambda b, i: (layer, 0, 0))
    row_spec = pl.BlockSpec((None, ROW_TILE, dc), lambda b, i: (b, i, 0))
    return pl.pallas_call(
        _conv_mix_kernel,
        grid=(bsz, n_tiles),
        in_specs=[
            row_spec,
            pl.BlockSpec((None, CONV_HALO, dc),
                         lambda b, i: (b, jnp.maximum(i * halo_per_tile - 1, 0), 0)),
            row_spec,
            pl.BlockSpec((None, CONV_WIDTH, dc), lambda b, i: (layer, 0, 0)),
            vec_spec, vec_spec, vec_spec,
            pl.BlockSpec((None, dc, dc), lambda b, i: (layer, 0, 0)),
            vec_spec,
        ],
        out_specs=row_spec,
        out_shape=jax.ShapeDtypeStruct((bsz, lp, dc), jnp.bfloat16),
        scratch_shapes=[pltpu.VMEM((ROW_TILE + CONV_HALO, dc), jnp.float32),
                        pltpu.VMEM((ROW_TILE, dc), jnp.bfloat16)],
        compiler_params=pltpu.CompilerParams(
            dimension_semantics=("parallel", "parallel"), vmem_limit_bytes=VMEM_LIMIT),
        name="conv_mix",
    )(glu, glu, cgate, conv_w, conv_b, ln_g, ln_b, w_pw2, b_pw2)


def _sb_attn_kernel(qt_ref, k_ref, vt_ref, sgate_ref, tri_ref, s_ref, acc_ref, carry_ref):
    blk = ATT_BLOCK
    i = pl.program_id(2)
    qt = qt_ref[...]
    row = lax.broadcasted_iota(jnp.int32, qt.shape, 0)
    zero = jnp.zeros_like(qt)
    q_heads = (jnp.where(row < HEAD_DIM, qt, zero), jnp.where(row >= HEAD_DIM, qt, zero))
    tri = tri_ref[...]

    acc_ref[...] = jnp.zeros_like(acc_ref)
    carry_ref[...] = jnp.zeros_like(carry_ref)

    def key_block(j, masked):
        jc = jnp.maximum(j, 0)
        start = pl.multiple_of(jc * blk, blk)
        kb = k_ref[pl.ds(start, blk), :]
        vb = vt_ref[jc]
        if masked:
            s_base = jnp.where(j >= 0, j * blk, (i + 2) * blk)
            s_pos = s_base + lax.broadcasted_iota(jnp.int32, (blk, blk), 0)
            t_pos = i * blk + lax.broadcasted_iota(jnp.int32, (blk, blk), 1)
            visible = s_pos < t_pos
        for hh in range(2):
            z = _dot(kb, q_heads[hh])
            log_stay = -(jnp.maximum(z, 0.0) + jnp.log1p(jnp.exp(-jnp.abs(z))))
            if masked:
                log_stay = jnp.where(visible, log_stay, 0.0)
            hi = log_stay.astype(jnp.bfloat16)
            lo = (log_stay - hi.astype(jnp.float32)).astype(jnp.bfloat16)
            sums = _dot(tri, jnp.concatenate([hi, lo], axis=0))
            tail = sums[0:blk]
            total = sums[blk:blk + 8]
            carry = carry_ref[hh]
            arg = (z + log_stay + tail).reshape(blk // 8, 8, blk) + carry[None]
            a = jnp.exp(arg).reshape(blk, blk)
            if masked:
                a = jnp.where(visible, a, 0.0)
            v_h = vb[hh * HEAD_DIM:(hh + 1) * HEAD_DIM, :]
            acc_ref[hh * HEAD_DIM:(hh + 1) * HEAD_DIM, :] += _dot(v_h, a.astype(jnp.bfloat16))
            carry_ref[hh] = carry + total

    for n in range(FIRST_STEP_BLOCKS):
        key_block(i - n, masked=True)

    def live():
        return (jnp.max(carry_ref[...]) >= LOG_F32_UNDERFLOW).astype(jnp.int32)

    def cond(c):
        j, go = c
        return jnp.logical_and(j >= 0, go > 0)

    def body(c):
        j, _ = c
        key_block(j, masked=False)
        return j - 1, live()

    lax.while_loop(cond, body, (i - FIRST_STEP_BLOCKS, live()))

    out = acc_ref[...].T
    s_ref[...] = (out * sgate_ref[...]).astype(jnp.bfloat16)


def _sb_attn(qt, k, vt, sgate, tri):
    bsz, n_blk, d_sb, blk = qt.shape
    lp = k.shape[1]
    n_pairs = d_sb // LANES
    return pl.pallas_call(
        _sb_attn_kernel,
        grid=(bsz, n_pairs, n_blk),
        in_specs=[
            pl.BlockSpec((None, None, LANES, blk), lambda b, p, i: (b, i, p, 0)),
            pl.BlockSpec((None, lp, LANES), lambda b, p, i: (b, 0, p)),
            pl.BlockSpec((None, n_blk, LANES, blk), lambda b, p, i: (b, 0, p, 0)),
            pl.BlockSpec((None, blk, LANES), lambda b, p, i: (b, i, p)),
            pl.BlockSpec(tri.shape, lambda b, p, i: (0, 0)),
        ],
        out_specs=pl.BlockSpec((None, blk, LANES), lambda b, p, i: (b, i, p)),
        out_shape=jax.ShapeDtypeStruct((bsz, lp, d_sb), jnp.bfloat16),
        scratch_shapes=[pltpu.VMEM((LANES, blk), jnp.float32),
                        pltpu.VMEM((2, 8, blk), jnp.float32)],
        compiler_params=pltpu.CompilerParams(
            dimension_semantics=("parallel", "parallel", "arbitrary"),
            vmem_limit_bytes=VMEM_LIMIT),
        name="sb_attn",
    )(qt, k, vt, sgate, tri)


def _out_proj_kernel(c_ref, s_ref, h_ref, w_ref, g_ref, o_ref, *, d_conv):
    mixed = _dot(c_ref[...], w_ref[0:d_conv, :]) + _dot(s_ref[...], w_ref[d_conv:, :])
    ms = jnp.mean(mixed * mixed, axis=-1, keepdims=True)
    o_ref[...] = h_ref[...] + mixed * lax.rsqrt(ms + RMS_EPS) * g_ref[...]


def _out_proj(c, s, h, w_out, post_g, layer):
    bsz, lp, d = h.shape
    d_conv = c.shape[-1]
    d_sb = s.shape[-1]
    n_tiles = lp // ROW_TILE
    return pl.pallas_call(
        functools.partial(_out_proj_kernel, d_conv=d_conv),
        grid=(bsz, n_tiles),
        in_specs=[
            pl.BlockSpec((None, ROW_TILE, d_conv), lambda b, i: (b, i, 0)),
            pl.BlockSpec((None, ROW_TILE, d_sb), lambda b, i: (b, i, 0)),
            pl.BlockSpec((None, ROW_TILE, d), lambda b, i: (b, i, 0)),
            pl.BlockSpec((None,) + w_out.shape[1:], lambda b, i: (layer, 0, 0)),
            pl.BlockSpec((None, 1, d), lambda b, i: (layer, 0, 0)),
        ],
        out_specs=pl.BlockSpec((None, ROW_TILE, d), lambda b, i: (b, i, 0)),
        out_shape=jax.ShapeDtypeStruct((bsz, lp, d), jnp.float32),
        compiler_params=pltpu.CompilerParams(
            dimension_semantics=("parallel", "parallel"), vmem_limit_bytes=VMEM_LIMIT),
        name="out_proj",
    )(c, s, h, w_out, post_g)


def _suffix_sum_matrix():
    r = lax.broadcasted_iota(jnp.int32, (TRI_ROWS, 2 * ATT_BLOCK), 0)
    c = lax.broadcasted_iota(jnp.int32, (TRI_ROWS, 2 * ATT_BLOCK), 1) % ATT_BLOCK
    return jnp.where((r >= ATT_BLOCK) | (c > r), 1.0, 0.0).astype(jnp.bfloat16)


def kernel(x, meta_tokens, pre_norm_g, post_norm_g, w_in, conv_w, conv_b, conv_ln_g, conv_ln_b,
           w_pw2, b_pw2, w_out):
    bsz, seq, d = x.shape
    depth = w_in.shape[0]
    d_conv = conv_w.shape[-1]
    d_sb = N_HEADS * HEAD_DIM
    assert w_in.shape[-1] == 3 * d_conv + 4 * d_sb and d_conv == d_sb
    assert ROW_TILE % ATT_BLOCK == 0 and ROW_TILE % CONV_HALO == 0 and ROW_TILE % CONV_CHUNK == 0

    length = N_META + seq
    lp = -(-length // ROW_TILE) * ROW_TILE
    meta = jnp.broadcast_to(meta_tokens.astype(x.dtype)[None], (bsz, N_META, d))
    h = jnp.concatenate([meta, x, jnp.zeros((bsz, lp - length, d), x.dtype)], axis=1)

    bf16 = jnp.bfloat16
    g = d_conv
    w_nat = jnp.concatenate([w_in[:, :, 0:3 * g], w_in[:, :, 4 * g:5 * g], w_in[:, :, 6 * g:7 * g]],
                            axis=-1).astype(bf16)
    w_t = jnp.swapaxes(jnp.concatenate([w_in[:, :, 3 * g:4 * g], w_in[:, :, 5 * g:6 * g]], axis=-1),
                       1, 2).astype(bf16)
    w_pw2_b = w_pw2.astype(bf16)
    w_out_b = w_out.astype(bf16)
    vec = lambda a: a.reshape(depth, 1, a.shape[-1])
    pre_g, post_g = vec(pre_norm_g), vec(post_norm_g)
    conv_b3, ln_g3, ln_b3, b_pw3 = vec(conv_b), vec(conv_ln_g), vec(conv_ln_b), vec(b_pw2)
    tri = _suffix_sum_matrix()

    for layer in range(depth):
        glu, cgate, k, sgate, qt, vt = _in_proj(h, pre_g, w_nat, w_t, layer)
        c = _conv_mix(glu, cgate, conv_w, conv_b3, ln_g3, ln_b3, w_pw2_b, b_pw3, layer)
        s = _sb_attn(qt, k, vt, sgate, tri)
        h = _out_proj(c, s, h, w_out_b, post_g, layer)
    return h[:, N_META:length]
```

```python
import functools

import jax
import jax.numpy as jnp
from jax import lax
from jax.experimental import pallas as pl
from jax.experimental.pallas import tpu as pltpu

N_META = 16
N_HEADS = 8
HEAD_DIM = 64
CONV_WIDTH = 31
RMS_EPS = 1e-6
LN_EPS = 1e-5

LANES = 128
ROW_TILE = 384
ATT_BLOCK = 128
FIRST_STEP_BLOCKS = 3
CONV_HALO = 32
CONV_CHUNK = 32
TRI_ROWS = ATT_BLOCK + 16
LOG_F32_UNDERFLOW = -104.0
VMEM_LIMIT = 48 * 1024 * 1024

_NT = (((1,), (1,)), ((), ()))


def _sigmoid(x):
    return 1.0 / (1.0 + jnp.exp(-x))


def _dot(a, b):
    return jnp.dot(a, b, preferred_element_type=jnp.float32)


def _in_proj_kernel(h_ref, g_ref, wn_ref, wt_ref,
                    glu_ref, cgate_ref, k_ref, sgate_ref, qt_ref, vt_ref, *, d_grp, n_sub):
    x = h_ref[...]
    ms = jnp.mean(x * x, axis=-1, keepdims=True)
    u = (x * lax.rsqrt(ms + RMS_EPS) * g_ref[...]).astype(jnp.bfloat16)

    a = _dot(u, wn_ref[:, 0 * d_grp:1 * d_grp])
    b = _dot(u, wn_ref[:, 1 * d_grp:2 * d_grp])
    glu_ref[...] = a * _sigmoid(b)
    cg = _dot(u, wn_ref[:, 2 * d_grp:3 * d_grp])
    cgate_ref[...] = cg * _sigmoid(cg)
    k_ref[...] = _dot(u, wn_ref[:, 3 * d_grp:4 * d_grp]).astype(jnp.bfloat16)
    sg = _dot(u, wn_ref[:, 4 * d_grp:5 * d_grp])
    sgate_ref[...] = sg * _sigmoid(sg)

    qt = lax.dot_general(wt_ref[0:d_grp, :], u, _NT, preferred_element_type=jnp.float32)
    qt = (qt * (HEAD_DIM ** -0.5)).astype(jnp.bfloat16)
    vt = lax.dot_general(wt_ref[d_grp:2 * d_grp, :], u, _NT,
                         preferred_element_type=jnp.float32).astype(jnp.bfloat16)
    for c in range(n_sub):
        qt_ref[c] = qt[:, c * ATT_BLOCK:(c + 1) * ATT_BLOCK]
        vt_ref[c] = vt[:, c * ATT_BLOCK:(c + 1) * ATT_BLOCK]


def _in_proj(h, pre_g, w_nat, w_t, layer):
    bsz, lp, d = h.shape
    d_grp = w_t.shape[1] // 2
    n_tiles = lp // ROW_TILE
    n_sub = ROW_TILE // ATT_BLOCK
    n_blk = lp // ATT_BLOCK
    row_f32 = jax.ShapeDtypeStruct((bsz, lp, d_grp), jnp.float32)
    row_bf16 = jax.ShapeDtypeStruct((bsz, lp, d_grp), jnp.bfloat16)
    fm_bf16 = jax.ShapeDtypeStruct((bsz, n_blk, d_grp, ATT_BLOCK), jnp.bfloat16)
    row_spec = pl.BlockSpec((None, ROW_TILE, d_grp), lambda b, i: (b, i, 0))
    fm_spec = pl.BlockSpec((None, n_sub, d_grp, ATT_BLOCK), lambda b, i: (b, i, 0, 0))
    return pl.pallas_call(
        functools.partial(_in_proj_kernel, d_grp=d_grp, n_sub=n_sub),
        grid=(bsz, n_tiles),
        in_specs=[
            pl.BlockSpec((None, ROW_TILE, d), lambda b, i: (b, i, 0)),
            pl.BlockSpec((None, 1, d), lambda b, i: (layer, 0, 0)),
            pl.BlockSpec((None,) + w_nat.shape[1:], lambda b, i: (layer, 0, 0)),
            pl.BlockSpec((None,) + w_t.shape[1:], lambda b, i: (layer, 0, 0)),
        ],
        out_specs=[row_spec, row_spec, row_spec, row_spec, fm_spec, fm_spec],
        out_shape=[row_f32, row_f32, row_bf16, row_f32, fm_bf16, fm_bf16],
        compiler_params=pltpu.CompilerParams(
            dimension_semantics=("parallel", "parallel"), vmem_limit_bytes=VMEM_LIMIT),
        name="in_proj",
    )(h, pre_g, w_nat, w_t)


def _conv_mix_kernel(cur_ref, halo_ref, cgate_ref, cw_ref, cb_ref, lg_ref, lb_ref, wp_ref, bp_ref,
                     c_ref, xs_ref, y_ref):
    i = pl.program_id(1)
    halo = halo_ref[...]
    xs_ref[0:CONV_HALO, :] = jnp.where(i > 0, halo, jnp.zeros_like(halo))
    xs_ref[CONV_HALO:, :] = cur_ref[...]

    cb = cb_ref[...]
    lg = lg_ref[...]
    lb = lb_ref[...]
    first = CONV_HALO - (CONV_WIDTH - 1)
    for r0 in range(0, ROW_TILE, CONV_CHUNK):
        acc = jnp.broadcast_to(cb, (CONV_CHUNK, cb.shape[-1]))
        for j in range(CONV_WIDTH):
            acc = acc + cw_ref[j:j + 1, :] * xs_ref[r0 + first + j:r0 + first + j + CONV_CHUNK, :]
        mu = jnp.mean(acc, axis=-1, keepdims=True)
        cen = acc - mu
        var = jnp.mean(cen * cen, axis=-1, keepdims=True)
        y = cen * lax.rsqrt(var + LN_EPS) * lg + lb
        y_ref[r0:r0 + CONV_CHUNK, :] = (y * _sigmoid(y)).astype(jnp.bfloat16)

    p = _dot(y_ref[...], wp_ref[...]) + bp_ref[...]
    c_ref[...] = (p * cgate_ref[...]).astype(jnp.bfloat16)


def _conv_mix(glu, cgate, conv_w, conv_b, ln_g, ln_b, w_pw2, b_pw2, layer):
    bsz, lp, dc = glu.shape
    n_tiles = lp // ROW_TILE
    halo_per_tile = ROW_TILE // CONV_HALO
    vec_spec = pl.BlockSpec((None, 1, dc), lambda b, i: (layer, 0, 0))
    row_spec = pl.BlockSpec((None, ROW_TILE, dc), lambda b, i: (b, i, 0))
    return pl.pallas_call(
        _conv_mix_kernel,
        grid=(bsz, n_tiles),
        in_specs=[
            row_spec,
            pl.BlockSpec((None, CONV_HALO, dc),
                         lambda b, i: (b, jnp.maximum(i * halo_per_tile - 1, 0), 0)),
            row_spec,
            pl.BlockSpec((None, CONV_WIDTH, dc), lambda b, i: (layer, 0, 0)),
            vec_spec, vec_spec, vec_spec,
            pl.BlockSpec((None, dc, dc), lambda b, i: (layer, 0, 0)),
            vec_spec,
        ],
        out_specs=row_spec,
        out_shape=jax.ShapeDtypeStruct((bsz, lp, dc), jnp.bfloat16),
        scratch_shapes=[pltpu.VMEM((ROW_TILE + CONV_HALO, dc), jnp.float32),
                        pltpu.VMEM((ROW_TILE, dc), jnp.bfloat16)],
        compiler_params=pltpu.CompilerParams(
            dimension_semantics=("parallel", "parallel"), vmem_limit_bytes=VMEM_LIMIT),
        name="conv_mix",
    )(glu, glu, cgate, conv_w, conv_b, ln_g, ln_b, w_pw2, b_pw2)


def _log_stay(z):
    return -(jnp.maximum(z, 0.0) + jnp.log(1.0 + jnp.exp(-jnp.abs(z))))


def _split_bf16(x):
    hi = x.astype(jnp.bfloat16)
    lo = (x - hi.astype(jnp.float32)).astype(jnp.bfloat16)
    return jnp.concatenate([hi, lo], axis=0)


def _add_rows8(x, c8):
    r, n = x.shape
    return (x.reshape(r // 8, 8, n) + c8[None]).reshape(r, n)


def _block_start(j):
    start = j * ATT_BLOCK
    return start if isinstance(start, int) else pl.multiple_of(start, ATT_BLOCK)


def _sb_attn_kernel(qt_ref, k_ref, vt_ref, sgate_ref, tri_ref, s_ref, acc_ref, carry_ref, *, n_blk):
    blk = ATT_BLOCK
    tri = tri_ref[...]
    row = lax.broadcasted_iota(jnp.int32, (LANES, blk), 0)
    head_rows = (row < HEAD_DIM, row >= HEAD_DIM)
    diag_visible = (lax.broadcasted_iota(jnp.int32, (blk, blk), 0)
                    < lax.broadcasted_iota(jnp.int32, (blk, blk), 1))

    def load_keys(j):
        return k_ref[pl.ds(_block_start(j), blk), :], vt_ref[j]

    def head_queries(i):
        qt = qt_ref[i]
        zero = jnp.zeros_like(qt)
        return [jnp.where(m, qt, zero) for m in head_rows]

    def write_out(i):
        rows = pl.ds(_block_start(i), blk)
        out = acc_ref[...].T
        s_ref[rows, :] = (out * sgate_ref[rows, :]).astype(jnp.bfloat16)

    def older_block(i, j):
        q_heads = head_queries(i)
        kb, vb = load_keys(j)
        for hh in range(2):
            z = _dot(kb, q_heads[hh])
            ls = _log_stay(z)
            sums = _dot(tri, _split_bf16(ls))
            carry = carry_ref[hh]
            a = jnp.exp(_add_rows8(z + ls + sums[0:blk], carry))
            v_h = vb[hh * HEAD_DIM:(hh + 1) * HEAD_DIM, :]
            acc_ref[hh * HEAD_DIM:(hh + 1) * HEAD_DIM, :] += _dot(v_h, a.astype(jnp.bfloat16))
            carry_ref[hh] = carry + sums[blk:blk + 8]

    def q_block(i, nb):
        q_heads = head_queries(i)
        keys = [load_keys(i - n) for n in range(nb)]
        chains = [(n, hh) for n in range(nb) for hh in range(2)]
        z = {c: _dot(keys[c[0]][0], q_heads[c[1]]) for c in chains}
        ls = {}
        for c in chains:
            l = _log_stay(z[c])
            ls[c] = jnp.where(diag_visible, l, 0.0) if c[0] == 0 else l
        sums = {c: _dot(tri, _split_bf16(ls[c])) for c in chains}
        a = {}
        carry = [None, None]
        for hh in range(2):
            for n in range(nb):
                c = (n, hh)
                arg = z[c] + ls[c] + sums[c][0:blk]
                if carry[hh] is not None:
                    arg = _add_rows8(arg, carry[hh])
                e = jnp.exp(arg)
                if n == 0:
                    e = jnp.where(diag_visible, e, 0.0)
                a[c] = e.astype(jnp.bfloat16)
                total = sums[c][blk:blk + 8]
                carry[hh] = total if carry[hh] is None else carry[hh] + total
        for hh in range(2):
            pv = None
            for n in range(nb):
                v_h = keys[n][1][hh * HEAD_DIM:(hh + 1) * HEAD_DIM, :]
                d = _dot(v_h, a[(n, hh)])
                pv = d if pv is None else pv + d
            acc_ref[hh * HEAD_DIM:(hh + 1) * HEAD_DIM, :] = pv
            carry_ref[hh] = carry[hh]
        write_out(i)

        if isinstance(i, int) and i - nb < 0:
            return
        live = jnp.max(jnp.maximum(carry[0], carry[1])) >= LOG_F32_UNDERFLOW

        @pl.when(jnp.logical_and(live, i - nb >= 0))
        def _():
            def cond(c):
                j, go = c
                return jnp.logical_and(j >= 0, go > 0)

            def body(c):
                j, _ = c
                older_block(i, j)
                go = (jnp.max(carry_ref[...]) >= LOG_F32_UNDERFLOW).astype(jnp.int32)
                return j - 1, go

            lax.while_loop(cond, body, (i - nb, jnp.int32(1)))
            write_out(i)

    lead = min(FIRST_STEP_BLOCKS - 1, n_blk)
    for i in range(lead):
        q_block(i, i + 1)

    def loop_body(i, c):
        q_block(i, FIRST_STEP_BLOCKS)
        return c

    lax.fori_loop(lead, n_blk, loop_body, 0)


def _sb_attn(qt, k, vt, sgate, tri):
    bsz, n_blk, d_sb, blk = qt.shape
    lp = k.shape[1]
    n_pairs = d_sb // LANES
    return pl.pallas_call(
        functools.partial(_sb_attn_kernel, n_blk=n_blk),
        grid=(bsz, n_pairs),
        in_specs=[
            pl.BlockSpec((None, n_blk, LANES, blk), lambda b, p: (b, 0, p, 0)),
            pl.BlockSpec((None, lp, LANES), lambda b, p: (b, 0, p)),
            pl.BlockSpec((None, n_blk, LANES, blk), lambda b, p: (b, 0, p, 0)),
            pl.BlockSpec((None, lp, LANES), lambda b, p: (b, 0, p)),
            pl.BlockSpec(tri.shape, lambda b, p: (0, 0)),
        ],
        out_specs=pl.BlockSpec((None, lp, LANES), lambda b, p: (b, 0, p)),
        out_shape=jax.ShapeDtypeStruct((bsz, lp, d_sb), jnp.bfloat16),
        scratch_shapes=[pltpu.VMEM((LANES, blk), jnp.float32),
                        pltpu.VMEM((2, 8, blk), jnp.float32)],
        compiler_params=pltpu.CompilerParams(
            dimension_semantics=("parallel", "parallel"), vmem_limit_bytes=VMEM_LIMIT),
        name="sb_attn",
    )(qt, k, vt, sgate, tri)


def _out_proj_kernel(c_ref, s_ref, h_ref, w_ref, g_ref, o_ref, *, d_conv):
    mixed = _dot(c_ref[...], w_ref[0:d_conv, :]) + _dot(s_ref[...], w_ref[d_conv:, :])
    ms = jnp.mean(mixed * mixed, axis=-1, keepdims=True)
    o_ref[...] = h_ref[...] + mixed * lax.rsqrt(ms + RMS_EPS) * g_ref[...]


def _out_proj(c, s, h, w_out, post_g, layer):
    bsz, lp, d = h.shape
    d_conv = c.shape[-1]
    d_sb = s.shape[-1]
    n_tiles = lp // ROW_TILE
    return pl.pallas_call(
        functools.partial(_out_proj_kernel, d_conv=d_conv),
        grid=(bsz, n_tiles),
        in_specs=[
            pl.BlockSpec((None, ROW_TILE, d_conv), lambda b, i: (b, i, 0)),
            pl.BlockSpec((None, ROW_TILE, d_sb), lambda b, i: (b, i, 0)),
            pl.BlockSpec((None, ROW_TILE, d), lambda b, i: (b, i, 0)),
            pl.BlockSpec((None,) + w_out.shape[1:], lambda b, i: (layer, 0, 0)),
            pl.BlockSpec((None, 1, d), lambda b, i: (layer, 0, 0)),
        ],
        out_specs=pl.BlockSpec((None, ROW_TILE, d), lambda b, i: (b, i, 0)),
        out_shape=jax.ShapeDtypeStruct((bsz, lp, d), jnp.float32),
        compiler_params=pltpu.CompilerParams(
            dimension_semantics=("parallel", "parallel"), vmem_limit_bytes=VMEM_LIMIT),
        name="out_proj",
    )(c, s, h, w_out, post_g)


def _suffix_sum_matrix():
    r = lax.broadcasted_iota(jnp.int32, (TRI_ROWS, 2 * ATT_BLOCK), 0)
    c = lax.broadcasted_iota(jnp.int32, (TRI_ROWS, 2 * ATT_BLOCK), 1) % ATT_BLOCK
    return jnp.where((r >= ATT_BLOCK) | (c > r), 1.0, 0.0).astype(jnp.bfloat16)


def kernel(x, meta_tokens, pre_norm_g, post_norm_g, w_in, conv_w, conv_b, conv_ln_g, conv_ln_b,
           w_pw2, b_pw2, w_out):
    bsz, seq, d = x.shape
    depth = w_in.shape[0]
    d_conv = conv_w.shape[-1]
    d_sb = N_HEADS * HEAD_DIM
    assert w_in.shape[-1] == 3 * d_conv + 4 * d_sb and d_conv == d_sb
    assert ROW_TILE % ATT_BLOCK == 0 and ROW_TILE % CONV_HALO == 0 and ROW_TILE % CONV_CHUNK == 0

    length = N_META + seq
    lp = -(-length // ROW_TILE) * ROW_TILE
    meta = jnp.broadcast_to(meta_tokens.astype(x.dtype)[None], (bsz, N_META, d))
    h = jnp.concatenate([meta, x, jnp.zeros((bsz, lp - length, d), x.dtype)], axis=1)

    bf16 = jnp.bfloat16
    g = d_conv
    w_nat = jnp.concatenate([w_in[:, :, 0:3 * g], w_in[:, :, 4 * g:5 * g], w_in[:, :, 6 * g:7 * g]],
                            axis=-1).astype(bf16)
    w_t = jnp.swapaxes(jnp.concatenate([w_in[:, :, 3 * g:4 * g], w_in[:, :, 5 * g:6 * g]], axis=-1),
                       1, 2).astype(bf16)
    w_pw2_b = w_pw2.astype(bf16)
    w_out_b = w_out.astype(bf16)
    vec = lambda a: a.reshape(depth, 1, a.shape[-1])
    pre_g, post_g = vec(pre_norm_g), vec(post_norm_g)
    conv_b3, ln_g3, ln_b3, b_pw3 = vec(conv_b), vec(conv_ln_g), vec(conv_ln_b), vec(b_pw2)
    tri = _suffix_sum_matrix()

    for layer in range(depth):
        glu, cgate, k, sgate, qt, vt = _in_proj(h, pre_g, w_nat, w_t, layer)
        c = _conv_mix(glu, cgate, conv_w, conv_b3, ln_g3, ln_b3, w_pw2_b, b_pw3, layer)
        s = _sb_attn(qt, k, vt, sgate, tri)
        h = _out_proj(c, s, h, w_out_b, post_g, layer)
    return h[:, N_META:length]
```

```python
import functools

import jax
import jax.numpy as jnp
from jax import lax
from jax.experimental import pallas as pl
from jax.experimental.pallas import tpu as pltpu

N_META = 16
N_HEADS = 8
HEAD_DIM = 64
CONV_WIDTH = 31
RMS_EPS = 1e-6
LN_EPS = 1e-5

LANES = 128
SUBLANES = 8
ROW_TILE = 384
ATT_BLOCK = 128
FIRST_STEP_BLOCKS = 3
ATT_UNROLL = 3
CONV_HALO = 32
CONV_CHUNK = 64
TRI_ROWS = ATT_BLOCK + 16
LOG_F32_UNDERFLOW = -104.0
VMEM_LIMIT = 48 * 1024 * 1024

_NT = (((1,), (1,)), ((), ()))


def _sigmoid(x):
    return 1.0 / (1.0 + jnp.exp(-x))


def _dot(a, b):
    return jnp.dot(a, b, preferred_element_type=jnp.float32)


def _in_proj_kernel(h_ref, g_ref, wn_ref, wt_ref,
                    glu_ref, cgate_ref, k_ref, sgate_ref, qt_ref, vt_ref, *, d_grp, n_sub):
    x = h_ref[...]
    ms = jnp.mean(x * x, axis=-1, keepdims=True)
    u = (x * lax.rsqrt(ms + RMS_EPS) * g_ref[...]).astype(jnp.bfloat16)

    a = _dot(u, wn_ref[:, 0 * d_grp:1 * d_grp])
    b = _dot(u, wn_ref[:, 1 * d_grp:2 * d_grp])
    glu_ref[...] = a * _sigmoid(b)
    cg = _dot(u, wn_ref[:, 2 * d_grp:3 * d_grp])
    cgate_ref[...] = cg * _sigmoid(cg)
    k_ref[...] = _dot(u, wn_ref[:, 3 * d_grp:4 * d_grp]).astype(jnp.bfloat16)
    sg = _dot(u, wn_ref[:, 4 * d_grp:5 * d_grp])
    sgate_ref[...] = sg * _sigmoid(sg)

    qt = lax.dot_general(wt_ref[0:d_grp, :], u, _NT, preferred_element_type=jnp.float32)
    qt = (qt * (HEAD_DIM ** -0.5)).astype(jnp.bfloat16)
    vt = lax.dot_general(wt_ref[d_grp:2 * d_grp, :], u, _NT,
                         preferred_element_type=jnp.float32).astype(jnp.bfloat16)
    for c in range(n_sub):
        qt_ref[c] = qt[:, c * ATT_BLOCK:(c + 1) * ATT_BLOCK]
        vt_ref[c] = vt[:, c * ATT_BLOCK:(c + 1) * ATT_BLOCK]


def _in_proj(h, pre_g, w_nat, w_t, layer):
    bsz, lp, d = h.shape
    d_grp = w_t.shape[1] // 2
    n_tiles = lp // ROW_TILE
    n_sub = ROW_TILE // ATT_BLOCK
    n_blk = lp // ATT_BLOCK
    row_f32 = jax.ShapeDtypeStruct((bsz, lp, d_grp), jnp.float32)
    row_bf16 = jax.ShapeDtypeStruct((bsz, lp, d_grp), jnp.bfloat16)
    fm_bf16 = jax.ShapeDtypeStruct((bsz, n_blk, d_grp, ATT_BLOCK), jnp.bfloat16)
    row_spec = pl.BlockSpec((None, ROW_TILE, d_grp), lambda b, i: (b, i, 0))
    fm_spec = pl.BlockSpec((None, n_sub, d_grp, ATT_BLOCK), lambda b, i: (b, i, 0, 0))
    return pl.pallas_call(
        functools.partial(_in_proj_kernel, d_grp=d_grp, n_sub=n_sub),
        grid=(bsz, n_tiles),
        in_specs=[
            pl.BlockSpec((None, ROW_TILE, d), lambda b, i: (b, i, 0)),
            pl.BlockSpec((None, 1, d), lambda b, i: (layer, 0, 0)),
            pl.BlockSpec((None,) + w_nat.shape[1:], lambda b, i: (layer, 0, 0)),
            pl.BlockSpec((None,) + w_t.shape[1:], lambda b, i: (layer, 0, 0)),
        ],
        out_specs=[row_spec, row_spec, row_spec, row_spec, fm_spec, fm_spec],
        out_shape=[row_f32, row_f32, row_bf16, row_f32, fm_bf16, fm_bf16],
        compiler_params=pltpu.CompilerParams(
            dimension_semantics=("parallel", "parallel"), vmem_limit_bytes=VMEM_LIMIT),
        name="in_proj",
    )(h, pre_g, w_nat, w_t)


def _conv_mix_kernel(cur_ref, halo_ref, cgate_ref, cw_ref, cb_ref, lg_ref, lb_ref, wp_ref, bp_ref,
                     c_ref, xs_ref, y_ref):
    i = pl.program_id(1)
    halo = halo_ref[...]
    xs_ref[0, 0:CONV_HALO, :] = jnp.where(i > 0, halo, jnp.zeros_like(halo))
    xs_ref[0, CONV_HALO:, :] = cur_ref[...]
    span = ROW_TILE + CONV_HALO - SUBLANES
    for r in range(1, SUBLANES):
        xs_ref[r, 0:span, :] = xs_ref[0, r:r + span, :]

    lg = lg_ref[...]
    lb = lb_ref[...]
    first = CONV_HALO - (CONV_WIDTH - 1)
    n_lane_cols = cb_ref.shape[-1] // LANES
    for r0 in range(0, ROW_TILE, CONV_CHUNK):
        cols = []
        for lc in range(n_lane_cols):
            lanes = slice(lc * LANES, (lc + 1) * LANES)
            acc = jnp.broadcast_to(cb_ref[:, lanes], (CONV_CHUNK, LANES))
            for j in range(CONV_WIDTH):
                tiles, r = divmod(first + j, SUBLANES)
                rows = slice(r0 + tiles * SUBLANES, r0 + tiles * SUBLANES + CONV_CHUNK)
                acc = acc + cw_ref[j:j + 1, lanes] * xs_ref[r, rows, lanes]
            cols.append(acc)
        acc = jnp.concatenate(cols, axis=-1)
        mu = jnp.mean(acc, axis=-1, keepdims=True)
        cen = acc - mu
        var = jnp.mean(cen * cen, axis=-1, keepdims=True)
        y = cen * lax.rsqrt(var + LN_EPS) * lg + lb
        y_ref[r0:r0 + CONV_CHUNK, :] = (y * _sigmoid(y)).astype(jnp.bfloat16)

    p = _dot(y_ref[...], wp_ref[...]) + bp_ref[...]
    c_ref[...] = (p * cgate_ref[...]).astype(jnp.bfloat16)


def _conv_mix(glu, cgate, conv_w, conv_b, ln_g, ln_b, w_pw2, b_pw2, layer):
    bsz, lp, dc = glu.shape
    n_tiles = lp // ROW_TILE
    halo_per_tile = ROW_TILE // CONV_HALO
    vec_spec = pl.BlockSpec((None, 1, dc), lambda b, i: (layer, 0, 0))
    row_spec = pl.BlockSpec((None, ROW_TILE, dc), lambda b, i: (b, i, 0))
    return pl.pallas_call(
        _conv_mix_kernel,
        grid=(bsz, n_tiles),
        in_specs=[
            row_spec,
            pl.BlockSpec((None, CONV_HALO, dc),
                         lambda b, i: (b, jnp.maximum(i * halo_per_tile - 1, 0), 0)),
            row_spec,
            pl.BlockSpec((None, CONV_WIDTH, dc), lambda b, i: (layer, 0, 0)),
            vec_spec, vec_spec, vec_spec,
            pl.BlockSpec((None, dc, dc), lambda b, i: (layer, 0, 0)),
            vec_spec,
        ],
        out_specs=row_spec,
        out_shape=jax.ShapeDtypeStruct((bsz, lp, dc), jnp.bfloat16),
        scratch_shapes=[pltpu.VMEM((SUBLANES, ROW_TILE + CONV_HALO, dc), jnp.float32),
                        pltpu.VMEM((ROW_TILE, dc), jnp.bfloat16)],
        compiler_params=pltpu.CompilerParams(
            dimension_semantics=("parallel", "parallel"), vmem_limit_bytes=VMEM_LIMIT),
        name="conv_mix",
    )(glu, glu, cgate, conv_w, conv_b, ln_g, ln_b, w_pw2, b_pw2)


def _log_stay(z):
    nz = -z
    return jnp.minimum(nz, 0.0) - jnp.log(1.0 + jnp.exp(jnp.minimum(z, nz)))


def _split_bf16(x):
    hi = x.astype(jnp.bfloat16)
    lo = (x - hi.astype(jnp.float32)).astype(jnp.bfloat16)
    return jnp.concatenate([hi, lo], axis=0)


def _add_rows8(x, c8):
    r, n = x.shape
    return (x.reshape(r // 8, 8, n) + c8[None]).reshape(r, n)


def _block_start(j):
    start = j * ATT_BLOCK
    return start if isinstance(start, int) else pl.multiple_of(start, ATT_BLOCK)


def _sb_attn_kernel(qt_ref, k_ref, vt_ref, sgate_ref, tri_ref, s_ref, acc_ref, carry_ref, *, n_blk):
    blk = ATT_BLOCK
    tri = tri_ref[...]
    row = lax.broadcasted_iota(jnp.int32, (LANES, blk), 0)
    head_rows = (row < HEAD_DIM, row >= HEAD_DIM)
    diag_visible = (lax.broadcasted_iota(jnp.int32, (blk, blk), 0)
                    < lax.broadcasted_iota(jnp.int32, (blk, blk), 1))
    head = lambda hh: slice(hh * HEAD_DIM, (hh + 1) * HEAD_DIM)
    lane_blk = lambda m: slice(m * blk, (m + 1) * blk)

    def load_keys(j):
        return k_ref[pl.ds(_block_start(j), blk), :], vt_ref[j]

    def head_queries(i):
        qt = qt_ref[i]
        zero = jnp.zeros_like(qt)
        return [jnp.where(m, qt, zero) for m in head_rows]

    def write_out(u, i):
        rows = pl.ds(_block_start(i), blk)
        out = acc_ref[u].T
        s_ref[rows, :] = (out * sgate_ref[rows, :]).astype(jnp.bfloat16)

    def older_block(u, i, j):
        q_heads = head_queries(i)
        kb, vb = load_keys(j)
        for hh in range(2):
            z = _dot(kb, q_heads[hh])
            ls = _log_stay(z)
            sums = _dot(tri, _split_bf16(ls))
            carry = carry_ref[u, hh]
            a = jnp.exp(_add_rows8(z + ls + sums[0:blk], carry))
            acc_ref[u, head(hh), :] += _dot(vb[head(hh), :], a.astype(jnp.bfloat16))
            carry_ref[u, hh] = carry + sums[blk:blk + 8]

    full = FIRST_STEP_BLOCKS

    def finish(base, n_q, live, stage):
        for u in range(n_q):
            i = base + u
            oldest = i - full
            if isinstance(oldest, int) and oldest < 0:
                continue
            if stage == 0:
                @pl.when(jnp.logical_and(live[u] > 0, oldest >= 0))
                def _():
                    def cond(c):
                        j, go = c
                        return jnp.logical_and(j >= 0, go > 0)

                    def body(c):
                        j, _ = c
                        older_block(u, i, j)
                        go = (jnp.max(carry_ref[u]) >= LOG_F32_UNDERFLOW).astype(jnp.int32)
                        return j - 1, go

                    lax.while_loop(cond, body, (jnp.asarray(oldest, jnp.int32), jnp.int32(1)))
        if stage == 1:
            for u in range(n_q):
                write_out(u, base + u)

    def q_blocks(base, depths, pending=None):
        n_q = len(depths)
        if pending is not None:
            finish(pending[0], n_q, pending[1], stage=0)
            finish(pending[0], n_q, pending[1], stage=1)
        q_heads = [head_queries(base + u) for u in range(n_q)]
        chains = [(u, n, hh) for u in range(n_q) for n in range(depths[u]) for hh in range(2)]
        offsets = sorted({u - n for u, n, _ in chains})
        keys = {d: load_keys(base + d) for d in offsets}

        z = {}
        for d in offsets:
            users = [c for c in chains if c[0] - c[1] == d]
            wide = _dot(keys[d][0], jnp.concatenate([q_heads[u][hh] for u, _, hh in users], axis=1))
            for m, c in enumerate(users):
                z[c] = wide[:, lane_blk(m)]

        ls = {}
        for c in chains:
            l = _log_stay(z[c])
            ls[c] = jnp.where(diag_visible, l, 0.0) if c[1] == 0 else l

        wide = _dot(tri, jnp.concatenate([_split_bf16(ls[c]) for c in chains], axis=1))
        sums = {c: wide[:, lane_blk(m)] for m, c in enumerate(chains)}

        last = {}
        for u in range(n_q):
            for hh in range(2):
                carry = None
                weights = []
                for n in range(depths[u]):
                    c = (u, n, hh)
                    arg = z[c] + ls[c] + sums[c][0:blk]
                    if carry is not None:
                        arg = _add_rows8(arg, carry)
                    e = jnp.exp(arg)
                    if n == 0:
                        e = jnp.where(diag_visible, e, 0.0)
                    weights.append(e.astype(jnp.bfloat16))
                    total = sums[c][blk:blk + 8]
                    carry = total if carry is None else carry + total
                values = [keys[u - n][1][head(hh), :] for n in range(depths[u])]
                acc_ref[u, head(hh), :] = _dot(jnp.concatenate(values, axis=1),
                                               jnp.concatenate(weights, axis=0))
                carry_ref[u, hh] = carry
                last[u, hh] = carry

        return tuple(
            (jnp.max(jnp.maximum(last[u, 0], last[u, 1])) >= LOG_F32_UNDERFLOW).astype(jnp.int32)
            for u in range(n_q))

    unroll = ATT_UNROLL
    lead = full - 1
    lead += (n_blk - lead) % unroll
    if lead < unroll:
        lead += unroll
    assert lead <= n_blk and (n_blk - lead) % unroll == 0
    starts = ([0] if lead % unroll else []) + list(range(lead % unroll, lead, unroll))
    for g, base in enumerate(starts):
        end = starts[g + 1] if g + 1 < len(starts) else lead
        live = q_blocks(base, [min(i + 1, full) for i in range(base, end)])
        if end < lead:
            finish(base, end - base, live, stage=0)
            finish(base, end - base, live, stage=1)

    def loop_body(t, live):
        base = lead + t * unroll
        return q_blocks(base, [full] * unroll, pending=(base - unroll, live))

    live = lax.fori_loop(0, (n_blk - lead) // unroll, loop_body, live)
    finish(n_blk - unroll, unroll, live, stage=0)
    finish(n_blk - unroll, unroll, live, stage=1)


def _sb_attn(qt, k, vt, sgate, tri):
    bsz, n_blk, d_sb, blk = qt.shape
    lp = k.shape[1]
    n_pairs = d_sb // LANES
    return pl.pallas_call(
        functools.partial(_sb_attn_kernel, n_blk=n_blk),
        grid=(bsz, n_pairs),
        in_specs=[
            pl.BlockSpec((None, n_blk, LANES, blk), lambda b, p: (b, 0, p, 0)),
            pl.BlockSpec((None, lp, LANES), lambda b, p: (b, 0, p)),
            pl.BlockSpec((None, n_blk, LANES, blk), lambda b, p: (b, 0, p, 0)),
            pl.BlockSpec((None, lp, LANES), lambda b, p: (b, 0, p)),
            pl.BlockSpec(tri.shape, lambda b, p: (0, 0)),
        ],
        out_specs=pl.BlockSpec((None, lp, LANES), lambda b, p: (b, 0, p)),
        out_shape=jax.ShapeDtypeStruct((bsz, lp, d_sb), jnp.bfloat16),
        scratch_shapes=[pltpu.VMEM((ATT_UNROLL, LANES, blk), jnp.float32),
                        pltpu.VMEM((ATT_UNROLL, 2, 8, blk), jnp.float32)],
        compiler_params=pltpu.CompilerParams(
            dimension_semantics=("parallel", "parallel"), vmem_limit_bytes=VMEM_LIMIT),
        name="sb_attn",
    )(qt, k, vt, sgate, tri)


def _out_proj_kernel(c_ref, s_ref, h_ref, w_ref, g_ref, o_ref, *, d_conv):
    mixed = _dot(c_ref[...], w_ref[0:d_conv, :]) + _dot(s_ref[...], w_ref[d_conv:, :])
    ms = jnp.mean(mixed * mixed, axis=-1, keepdims=True)
    o_ref[...] = h_ref[...] + mixed * lax.rsqrt(ms + RMS_EPS) * g_ref[...]


def _out_proj(c, s, h, w_out, post_g, layer):
    bsz, lp, d = h.shape
    d_conv = c.shape[-1]
    d_sb = s.shape[-1]
    n_tiles = lp // ROW_TILE
    return pl.pallas_call(
        functools.partial(_out_proj_kernel, d_conv=d_conv),
        grid=(bsz, n_tiles),
        in_specs=[
            pl.BlockSpec((None, ROW_TILE, d_conv), lambda b, i: (b, i, 0)),
            pl.BlockSpec((None, ROW_TILE, d_sb), lambda b, i: (b, i, 0)),
            pl.BlockSpec((None, ROW_TILE, d), lambda b, i: (b, i, 0)),
            pl.BlockSpec((None,) + w_out.shape[1:], lambda b, i: (layer, 0, 0)),
            pl.BlockSpec((None, 1, d), lambda b, i: (layer, 0, 0)),
        ],
        out_specs=pl.BlockSpec((None, ROW_TILE, d), lambda b, i: (b, i, 0)),
        out_shape=jax.ShapeDtypeStruct((bsz, lp, d), jnp.float32),
        compiler_params=pltpu.CompilerParams(
            dimension_semantics=("parallel", "parallel"), vmem_limit_bytes=VMEM_LIMIT),
        name="out_proj",
    )(c, s, h, w_out, post_g)


def _suffix_sum_matrix():
    r = lax.broadcasted_iota(jnp.int32, (TRI_ROWS, 2 * ATT_BLOCK), 0)
    c = lax.broadcasted_iota(jnp.int32, (TRI_ROWS, 2 * ATT_BLOCK), 1) % ATT_BLOCK
    return jnp.where((r >= ATT_BLOCK) | (c > r), 1.0, 0.0).astype(jnp.bfloat16)


def kernel(x, meta_tokens, pre_norm_g, post_norm_g, w_in, conv_w, conv_b, conv_ln_g, conv_ln_b,
           w_pw2, b_pw2, w_out):
    bsz, seq, d = x.shape
    depth = w_in.shape[0]
    d_conv = conv_w.shape[-1]
    d_sb = N_HEADS * HEAD_DIM
    assert w_in.shape[-1] == 3 * d_conv + 4 * d_sb and d_conv == d_sb
    assert ROW_TILE % ATT_BLOCK == 0 and ROW_TILE % CONV_HALO == 0 and ROW_TILE % CONV_CHUNK == 0

    length = N_META + seq
    lp = -(-length // ROW_TILE) * ROW_TILE
    meta = jnp.broadcast_to(meta_tokens.astype(x.dtype)[None], (bsz, N_META, d))
    h = jnp.concatenate([meta, x, jnp.zeros((bsz, lp - length, d), x.dtype)], axis=1)

    bf16 = jnp.bfloat16
    g = d_conv
    w_nat = jnp.concatenate([w_in[:, :, 0:3 * g], w_in[:, :, 4 * g:5 * g], w_in[:, :, 6 * g:7 * g]],
                            axis=-1).astype(bf16)
    w_t = jnp.swapaxes(jnp.concatenate([w_in[:, :, 3 * g:4 * g], w_in[:, :, 5 * g:6 * g]], axis=-1),
                       1, 2).astype(bf16)
    w_pw2_b = w_pw2.astype(bf16)
    w_out_b = w_out.astype(bf16)
    vec = lambda a: a.reshape(depth, 1, a.shape[-1])
    pre_g, post_g = vec(pre_norm_g), vec(post_norm_g)
    conv_b3, ln_g3, ln_b3, b_pw3 = vec(conv_b), vec(conv_ln_g), vec(conv_ln_b), vec(b_pw2)
    tri = _suffix_sum_matrix()

    for layer in range(depth):
        glu, cgate, k, sgate, qt, vt = _in_proj(h, pre_g, w_nat, w_t, layer)
        c = _conv_mix(glu, cgate, conv_w, conv_b3, ln_g3, ln_b3, w_pw2_b, b_pw3, layer)
        s = _sb_attn(qt, k, vt, sgate, tri)
        h = _out_proj(c, s, h, w_out_b, post_g, layer)
    return h[:, N_META:length]
```

```python
import functools

import jax
import jax.numpy as jnp
from jax import lax
from jax.experimental import pallas as pl
from jax.experimental.pallas import tpu as pltpu

N_META = 16
N_HEADS = 8
HEAD_DIM = 64
CONV_WIDTH = 31
RMS_EPS = 1e-6
LN_EPS = 1e-5

LANES = 128
SUBLANES = 8
ROW_TILE = 384
ATT_BLOCK = 128
FIRST_STEP_BLOCKS = 3
OLDEST_ROWS = 32
ATT_UNROLL = 6
CONV_HALO = 32
CONV_CHUNK = 64
TRI_ROWS = ATT_BLOCK + 16
LOG_F32_UNDERFLOW = -104.0
VMEM_LIMIT = 48 * 1024 * 1024
VMEM_LIMIT_MIX_OUT = 58 * 1024 * 1024


def _sigmoid(x):
    return 1.0 / (1.0 + jnp.exp(-x))


def _dot(a, b):
    return jnp.dot(a, b, preferred_element_type=jnp.float32)


def _project(x, g_ref, w_ref, glu_ref, cgate_ref, k_ref, sgate_ref, qt_ref, vt_ref):
    d_grp = glu_ref.shape[-1]
    ms = jnp.mean(x * x, axis=-1, keepdims=True)
    u = (x * lax.rsqrt(ms + RMS_EPS) * g_ref[...]).astype(jnp.bfloat16)
    group = lambda m: _dot(u, w_ref[:, m * d_grp:(m + 1) * d_grp])

    glu_ref[...] = group(0) * _sigmoid(group(1))
    cg = group(2)
    cgate_ref[...] = cg * _sigmoid(cg)
    k_ref[...] = group(4).astype(jnp.bfloat16)
    sg = group(6)
    sgate_ref[...] = sg * _sigmoid(sg)

    qt = (group(3) * (HEAD_DIM ** -0.5)).T.astype(jnp.bfloat16)
    vt = group(5).T.astype(jnp.bfloat16)
    for c in range(qt_ref.shape[0]):
        qt_ref[c] = qt[:, c * ATT_BLOCK:(c + 1) * ATT_BLOCK]
        vt_ref[c] = vt[:, c * ATT_BLOCK:(c + 1) * ATT_BLOCK]


def _in_proj_kernel(h_ref, g_ref, w_ref, *out_refs):
    _project(h_ref[...], g_ref, w_ref, *out_refs)


def _projection_outputs(bsz, lp, d_grp):
    n_sub = ROW_TILE // ATT_BLOCK
    row_f32 = jax.ShapeDtypeStruct((bsz, lp, d_grp), jnp.float32)
    row_bf16 = jax.ShapeDtypeStruct((bsz, lp, d_grp), jnp.bfloat16)
    fm_bf16 = jax.ShapeDtypeStruct((bsz, lp // ATT_BLOCK, d_grp, ATT_BLOCK), jnp.bfloat16)
    row_spec = pl.BlockSpec((None, ROW_TILE, d_grp), lambda b, i: (b, i, 0))
    fm_spec = pl.BlockSpec((None, n_sub, d_grp, ATT_BLOCK), lambda b, i: (b, i, 0, 0))
    return ([row_f32, row_f32, row_bf16, row_f32, fm_bf16, fm_bf16],
            [row_spec, row_spec, row_spec, row_spec, fm_spec, fm_spec])


def _projection_inputs(pre_g, w_in, layer):
    d = pre_g.shape[-1]
    return ([pre_g, w_in],
            [pl.BlockSpec((None, 1, d), lambda b, i: (layer, 0, 0)),
             pl.BlockSpec((None,) + w_in.shape[1:], lambda b, i: (layer, 0, 0))])


def _in_proj(h, pre_g, w_in, d_grp, layer):
    bsz, lp, d = h.shape
    out_shape, out_specs = _projection_outputs(bsz, lp, d_grp)
    operands, specs = _projection_inputs(pre_g, w_in, layer)
    return pl.pallas_call(
        _in_proj_kernel,
        grid=(bsz, lp // ROW_TILE),
        in_specs=[pl.BlockSpec((None, ROW_TILE, d), lambda b, i: (b, i, 0))] + specs,
        out_specs=out_specs,
        out_shape=out_shape,
        compiler_params=pltpu.CompilerParams(
            dimension_semantics=("parallel", "parallel"), vmem_limit_bytes=VMEM_LIMIT),
        name="in_proj",
    )(h, *operands)


def _conv_branch(cur_ref, halo_ref, cgate_ref, cw_ref, cb_ref, lg_ref, lb_ref, wp_ref, bp_ref,
                 xs_ref, y_ref):
    i = pl.program_id(1)
    halo = halo_ref[...]
    xs_ref[0, 0:CONV_HALO, :] = jnp.where(i > 0, halo, jnp.zeros_like(halo))
    xs_ref[0, CONV_HALO:, :] = cur_ref[...]
    span = ROW_TILE + CONV_HALO - SUBLANES
    for r in range(1, SUBLANES):
        xs_ref[r, 0:span, :] = xs_ref[0, r:r + span, :]

    lg = lg_ref[...]
    lb = lb_ref[...]
    first = CONV_HALO - (CONV_WIDTH - 1)
    n_lane_cols = cb_ref.shape[-1] // LANES
    for r0 in range(0, ROW_TILE, CONV_CHUNK):
        cols = []
        for lc in range(n_lane_cols):
            lanes = slice(lc * LANES, (lc + 1) * LANES)
            acc = jnp.broadcast_to(cb_ref[:, lanes], (CONV_CHUNK, LANES))
            for j in range(CONV_WIDTH):
                tiles, r = divmod(first + j, SUBLANES)
                rows = slice(r0 + tiles * SUBLANES, r0 + tiles * SUBLANES + CONV_CHUNK)
                acc = acc + cw_ref[j:j + 1, lanes] * xs_ref[r, rows, lanes]
            cols.append(acc)
        acc = jnp.concatenate(cols, axis=-1)
        mu = jnp.mean(acc, axis=-1, keepdims=True)
        cen = acc - mu
        var = jnp.mean(cen * cen, axis=-1, keepdims=True)
        y = cen * lax.rsqrt(var + LN_EPS) * lg + lb
        y_ref[r0:r0 + CONV_CHUNK, :] = (y * _sigmoid(y)).astype(jnp.bfloat16)

    p = _dot(y_ref[...], wp_ref[...]) + bp_ref[...]
    return (p * cgate_ref[...]).astype(jnp.bfloat16)


def _mix_out_kernel(*refs, has_next):
    (cur_ref, halo_ref, cgate_ref, cw_ref, cb_ref, lg_ref, lb_ref, wp_ref, bp_ref,
     s_ref, h_ref, wo_ref, pg_ref) = refs[:13]
    rest = refs[13:]
    next_in, rest = (rest[:2], rest[2:]) if has_next else ((), rest)
    o_ref = rest[0]
    next_out = rest[1:7] if has_next else ()
    xs_ref, y_ref = rest[-2:]

    c = _conv_branch(cur_ref, halo_ref, cgate_ref, cw_ref, cb_ref, lg_ref, lb_ref, wp_ref, bp_ref,
                     xs_ref, y_ref)
    d_conv = c.shape[-1]
    mixed = _dot(c, wo_ref[0:d_conv, :]) + _dot(s_ref[...], wo_ref[d_conv:, :])
    ms = jnp.mean(mixed * mixed, axis=-1, keepdims=True)
    h_new = h_ref[...] + mixed * lax.rsqrt(ms + RMS_EPS) * pg_ref[...]
    o_ref[...] = h_new
    if has_next:
        _project(h_new, *next_in, *next_out)


def _mix_out(glu, cgate, s, h, conv_w, conv_b, ln_g, ln_b, w_pw2, b_pw2, w_out, post_g, layer,
             next_proj=None):
    bsz, lp, dc = glu.shape
    d = h.shape[-1]
    halo_per_tile = ROW_TILE // CONV_HALO
    vec_spec = pl.BlockSpec((None, 1, dc), lambda b, i: (layer, 0, 0))
    row_spec = pl.BlockSpec((None, ROW_TILE, dc), lambda b, i: (b, i, 0))
    h_spec = pl.BlockSpec((None, ROW_TILE, d), lambda b, i: (b, i, 0))
    operands = [glu, glu, cgate, conv_w, conv_b, ln_g, ln_b, w_pw2, b_pw2, s, h, w_out, post_g]
    in_specs = [
        row_spec,
        pl.BlockSpec((None, CONV_HALO, dc),
                     lambda b, i: (b, jnp.maximum(i * halo_per_tile - 1, 0), 0)),
        row_spec,
        pl.BlockSpec((None, CONV_WIDTH, dc), lambda b, i: (layer, 0, 0)),
        vec_spec, vec_spec, vec_spec,
        pl.BlockSpec((None, dc, dc), lambda b, i: (layer, 0, 0)),
        vec_spec,
        pl.BlockSpec((None, ROW_TILE, s.shape[-1]), lambda b, i: (b, i, 0)),
        h_spec,
        pl.BlockSpec((None,) + w_out.shape[1:], lambda b, i: (layer, 0, 0)),
        pl.BlockSpec((None, 1, d), lambda b, i: (layer, 0, 0)),
    ]
    out_shape = [jax.ShapeDtypeStruct((bsz, lp, d), jnp.float32)]
    out_specs = [h_spec]
    if next_proj is not None:
        more_operands, more_specs = _projection_inputs(*next_proj, layer + 1)
        operands += more_operands
        in_specs += more_specs
        proj_shape, proj_specs = _projection_outputs(bsz, lp, dc)
        out_shape += proj_shape
        out_specs += proj_specs
    return pl.pallas_call(
        functools.partial(_mix_out_kernel, has_next=next_proj is not None),
        grid=(bsz, lp // ROW_TILE),
        in_specs=in_specs,
        out_specs=out_specs,
        out_shape=out_shape,
        scratch_shapes=[pltpu.VMEM((SUBLANES, ROW_TILE + CONV_HALO, dc), jnp.float32),
                        pltpu.VMEM((ROW_TILE, dc), jnp.bfloat16)],
        compiler_params=pltpu.CompilerParams(
            dimension_semantics=("parallel", "parallel"), vmem_limit_bytes=VMEM_LIMIT_MIX_OUT),
        name="mix_out",
    )(*operands)


def _log_gates(z):
    log_go = jnp.minimum(z, 0.0) - jnp.log(1.0 + jnp.exp(jnp.minimum(z, -z)))
    return log_go, log_go - z


def _split_bf16(x):
    hi = x.astype(jnp.bfloat16)
    lo = (x - hi.astype(jnp.float32)).astype(jnp.bfloat16)
    return hi, lo


def _add_rows8(x, c8):
    r, n = x.shape
    return (x.reshape(r // 8, 8, n) + c8[None]).reshape(r, n)


def _block_start(j):
    start = j * ATT_BLOCK
    return start if isinstance(start, int) else pl.multiple_of(start, ATT_BLOCK)


def _sb_attn_kernel(qt_ref, k_ref, vt_ref, sgate_ref, tri_ref, s_ref, acc_ref, carry_ref, *, n_blk):
    blk = ATT_BLOCK
    full = FIRST_STEP_BLOCKS
    cut = blk - OLDEST_ROWS
    tri = tri_ref[...]
    row = lax.broadcasted_iota(jnp.int32, (LANES, blk), 0)
    head_rows = (row < HEAD_DIM, row >= HEAD_DIM)
    key_row = lax.broadcasted_iota(jnp.int32, (blk, blk), 0)
    diag_visible = key_row < lax.broadcasted_iota(jnp.int32, (blk, blk), 1)
    head = lambda hh: slice(hh * HEAD_DIM, (hh + 1) * HEAD_DIM)
    lane_blk = lambda m: slice(m * blk, (m + 1) * blk)
    no_rows = jnp.zeros((cut, blk), jnp.bfloat16)

    def load_keys(j):
        return k_ref[pl.ds(_block_start(j), blk), :], vt_ref[j]

    def head_queries(i):
        qt = qt_ref[i]
        zero = jnp.zeros_like(qt)
        return [jnp.where(m, qt, zero) for m in head_rows]

    def write_out(u, i):
        rows = pl.ds(_block_start(i), blk)
        out = acc_ref[u].T
        s_ref[rows, :] = (out * sgate_ref[rows, :]).astype(jnp.bfloat16)

    def older_rows(u, i, j, n_rows):
        q_heads = head_queries(i)
        kb, vb = load_keys(j)
        wanted = key_row < n_rows
        for hh in range(2):
            z = _dot(kb, q_heads[hh])
            log_go, log_stay = _log_gates(z)
            log_stay = jnp.where(wanted, log_stay, 0.0)
            sums = _dot(tri, jnp.concatenate(_split_bf16(log_stay), axis=0))
            carry = carry_ref[u, hh]
            a = jnp.where(wanted, jnp.exp(_add_rows8(log_go + sums[0:blk], carry)), 0.0)
            acc_ref[u, head(hh), :] += _dot(vb[head(hh), :], a.astype(jnp.bfloat16))
            carry_ref[u, hh] = carry + sums[blk:blk + 8]

    def finish(base, n_q, live, stage):
        for u in range(n_q):
            i = base + u
            oldest = i - (full - 1)
            if isinstance(oldest, int) and oldest < 0:
                continue
            if stage == 0:
                @pl.when(jnp.logical_and(live[u] > 0, oldest >= 0))
                def _():
                    def cond(c):
                        j, go = c
                        return jnp.logical_and(j >= 0, go > 0)

                    def body(c):
                        j, _ = c
                        older_rows(u, i, j, jnp.where(j == oldest, cut, blk))
                        go = (jnp.max(carry_ref[u]) >= LOG_F32_UNDERFLOW).astype(jnp.int32)
                        return j - 1, go

                    lax.while_loop(cond, body, (jnp.asarray(oldest, jnp.int32), jnp.int32(1)))
        if stage == 1:
            for u in range(n_q):
                write_out(u, base + u)

    def q_blocks(base, depths, pending=None):
        n_q = len(depths)
        if pending is not None:
            finish(pending[0], n_q, pending[1], stage=0)
            finish(pending[0], n_q, pending[1], stage=1)
        q_heads = [head_queries(base + u) for u in range(n_q)]
        chains = [(u, n, hh) for u in range(n_q) for n in range(depths[u]) for hh in range(2)]
        partial = lambda c: c[1] == full - 1
        offsets = sorted({u - n for u, n, _ in chains})
        keys = {d: load_keys(base + d) for d in offsets}

        z = {}
        for d in offsets:
            users = [c for c in chains if c[0] - c[1] == d]
            wide = _dot(keys[d][0], jnp.concatenate([q_heads[u][hh] for u, _, hh in users], axis=1))
            for m, c in enumerate(users):
                z[c] = wide[cut:, lane_blk(m)] if partial(c) else wide[:, lane_blk(m)]

        log_go, log_stay, halves = {}, {}, []
        for c in chains:
            log_go[c], stay = _log_gates(z[c])
            if c[1] == 0:
                stay = jnp.where(diag_visible, stay, 0.0)
            log_stay[c] = stay
            hi, lo = _split_bf16(stay)
            halves.append(jnp.concatenate([no_rows, hi, no_rows, lo] if partial(c) else [hi, lo],
                                          axis=0))

        wide = _dot(tri, jnp.concatenate(halves, axis=1))
        sums = {c: wide[:, lane_blk(m)] for m, c in enumerate(chains)}

        last = {}
        for u in range(n_q):
            for hh in range(2):
                carry = None
                weights = []
                for n in range(depths[u]):
                    c = (u, n, hh)
                    arg = log_go[c] + (sums[c][cut:blk] if partial(c) else sums[c][0:blk])
                    if carry is not None:
                        arg = _add_rows8(arg, carry)
                    e = jnp.exp(arg)
                    if n == 0:
                        e = jnp.where(diag_visible, e, 0.0)
                    if partial(c):
                        weights.append(no_rows)
                    weights.append(e.astype(jnp.bfloat16))
                    total = sums[c][blk:blk + 8]
                    carry = total if carry is None else carry + total
                values = [keys[u - n][1][head(hh), :] for n in range(depths[u])]
                acc_ref[u, head(hh), :] = _dot(jnp.concatenate(values, axis=1),
                                               jnp.concatenate(weights, axis=0))
                carry_ref[u, hh] = carry
                last[u, hh] = carry

        return tuple(
            (jnp.max(jnp.maximum(last[u, 0], last[u, 1])) >= LOG_F32_UNDERFLOW).astype(jnp.int32)
            for u in range(n_q))

    unroll = ATT_UNROLL
    lead = full - 1
    lead += (n_blk - lead) % unroll
    if lead < unroll:
        lead += unroll
    assert lead <= n_blk and (n_blk - lead) % unroll == 0
    starts = ([0] if lead % unroll else []) + list(range(lead % unroll, lead, unroll))
    for g, base in enumerate(starts):
        end = starts[g + 1] if g + 1 < len(starts) else lead
        live = q_blocks(base, [min(i + 1, full) for i in range(base, end)])
        if end < lead:
            finish(base, end - base, live, stage=0)
            finish(base, end - base, live, stage=1)

    def loop_body(t, live):
        base = lead + t * unroll
        return q_blocks(base, [full] * unroll, pending=(base - unroll, live))

    live = lax.fori_loop(0, (n_blk - lead) // unroll, loop_body, live)
    finish(n_blk - unroll, unroll, live, stage=0)
    finish(n_blk - unroll, unroll, live, stage=1)


def _sb_attn(qt, k, vt, sgate, tri):
    bsz, n_blk, d_sb, blk = qt.shape
    lp = k.shape[1]
    n_pairs = d_sb // LANES
    return pl.pallas_call(
        functools.partial(_sb_attn_kernel, n_blk=n_blk),
        grid=(bsz, n_pairs),
        in_specs=[
            pl.BlockSpec((None, n_blk, LANES, blk), lambda b, p: (b, 0, p, 0)),
            pl.BlockSpec((None, lp, LANES), lambda b, p: (b, 0, p)),
            pl.BlockSpec((None, n_blk, LANES, blk), lambda b, p: (b, 0, p, 0)),
            pl.BlockSpec((None, lp, LANES), lambda b, p: (b, 0, p)),
            pl.BlockSpec(tri.shape, lambda b, p: (0, 0)),
        ],
        out_specs=pl.BlockSpec((None, lp, LANES), lambda b, p: (b, 0, p)),
        out_shape=jax.ShapeDtypeStruct((bsz, lp, d_sb), jnp.bfloat16),
        scratch_shapes=[pltpu.VMEM((ATT_UNROLL, LANES, blk), jnp.float32),
                        pltpu.VMEM((ATT_UNROLL, 2, 8, blk), jnp.float32)],
        compiler_params=pltpu.CompilerParams(
            dimension_semantics=("parallel", "parallel"), vmem_limit_bytes=VMEM_LIMIT),
        name="sb_attn",
    )(qt, k, vt, sgate, tri)


def _suffix_sum_matrix():
    r = lax.broadcasted_iota(jnp.int32, (TRI_ROWS, 2 * ATT_BLOCK), 0)
    c = lax.broadcasted_iota(jnp.int32, (TRI_ROWS, 2 * ATT_BLOCK), 1) % ATT_BLOCK
    return jnp.where((r >= ATT_BLOCK) | (c > r), 1.0, 0.0).astype(jnp.bfloat16)


def kernel(x, meta_tokens, pre_norm_g, post_norm_g, w_in, conv_w, conv_b, conv_ln_g, conv_ln_b,
           w_pw2, b_pw2, w_out):
    bsz, seq, d = x.shape
    depth = w_in.shape[0]
    d_conv = conv_w.shape[-1]
    d_sb = N_HEADS * HEAD_DIM
    assert w_in.shape[-1] == 3 * d_conv + 4 * d_sb and d_conv == d_sb
    assert ROW_TILE % ATT_BLOCK == 0 and ROW_TILE % CONV_CHUNK == 0
    assert CONV_HALO % SUBLANES == 0 and CONV_HALO >= CONV_WIDTH - 1 and ROW_TILE % CONV_HALO == 0

    length = N_META + seq
    lp = -(-length // ROW_TILE) * ROW_TILE
    meta = jnp.broadcast_to(meta_tokens.astype(x.dtype)[None], (bsz, N_META, d))
    h = jnp.concatenate([meta, x, jnp.zeros((bsz, lp - length, d), x.dtype)], axis=1)

    bf16 = jnp.bfloat16
    w_in_b = w_in.astype(bf16)
    w_pw2_b = w_pw2.astype(bf16)
    w_out_b = w_out.astype(bf16)
    vec = lambda a: a.reshape(depth, 1, a.shape[-1])
    pre_g, post_g = vec(pre_norm_g), vec(post_norm_g)
    conv_b3, ln_g3, ln_b3, b_pw3 = vec(conv_b), vec(conv_ln_g), vec(conv_ln_b), vec(b_pw2)
    tri = _suffix_sum_matrix()

    glu, cgate, k, sgate, qt, vt = _in_proj(h, pre_g, w_in_b, d_conv, 0)
    for layer in range(depth):
        s = _sb_attn(qt, k, vt, sgate, tri)
        next_proj = (pre_g, w_in_b) if layer + 1 < depth else None
        outs = _mix_out(glu, cgate, s, h, conv_w, conv_b3, ln_g3, ln_b3, w_pw2_b, b_pw3,
                        w_out_b, post_g, layer, next_proj)
        h = outs[0]
        if next_proj is not None:
            glu, cgate, k, sgate, qt, vt = outs[1:]
    return h[:, N_META:length]
```

```python
import functools

import jax
import jax.numpy as jnp
from jax import lax
from jax.experimental import pallas as pl
from jax.experimental.pallas import tpu as pltpu

N_META = 16
N_HEADS = 8
HEAD_DIM = 64
CONV_WIDTH = 31
RMS_EPS = 1e-6
LN_EPS = 1e-5

LANES = 128
SUBLANES = 8
ROW_TILE = 384
ATT_BLOCK = 128
FIRST_STEP_BLOCKS = 3
OLDEST_ROWS = 32
ATT_UNROLL = 6
CONV_HALO = 32
CONV_CHUNK = 64
TRI_ROWS = ATT_BLOCK + 16
LOG_F32_UNDERFLOW = -104.0
VMEM_LIMIT = 48 * 1024 * 1024
VMEM_LIMIT_MIX_OUT = 58 * 1024 * 1024


def _sigmoid(x):
    return 1.0 / (1.0 + jnp.exp(-x))


def _dot(a, b):
    return jnp.dot(a, b, preferred_element_type=jnp.float32)


def _project(x, g_ref, w_ref, glu_ref, cgate_ref, k_ref, sgate_ref, qt_ref, vt_ref):
    d_grp = glu_ref.shape[-1]
    ms = jnp.mean(x * x, axis=-1, keepdims=True)
    u = (x * lax.rsqrt(ms + RMS_EPS) * g_ref[...]).astype(jnp.bfloat16)
    group = lambda m: _dot(u, w_ref[:, m * d_grp:(m + 1) * d_grp])

    glu_ref[...] = group(0) * _sigmoid(group(1))
    cg = group(2)
    cgate_ref[...] = cg * _sigmoid(cg)
    k_ref[...] = group(4).astype(jnp.bfloat16)
    sg = group(6)
    sgate_ref[...] = sg * _sigmoid(sg)

    qt = (group(3) * (HEAD_DIM ** -0.5)).T.astype(jnp.bfloat16)
    vt = group(5).T.astype(jnp.bfloat16)
    for c in range(qt_ref.shape[0]):
        qt_ref[c] = qt[:, c * ATT_BLOCK:(c + 1) * ATT_BLOCK]
        vt_ref[c] = vt[:, c * ATT_BLOCK:(c + 1) * ATT_BLOCK]


def _in_proj_kernel(h_ref, g_ref, w_ref, *out_refs):
    _project(h_ref[...], g_ref, w_ref, *out_refs)


def _projection_outputs(bsz, lp, d_grp):
    n_sub = ROW_TILE // ATT_BLOCK
    row_f32 = jax.ShapeDtypeStruct((bsz, lp, d_grp), jnp.float32)
    row_bf16 = jax.ShapeDtypeStruct((bsz, lp, d_grp), jnp.bfloat16)
    fm_bf16 = jax.ShapeDtypeStruct((bsz, lp // ATT_BLOCK, d_grp, ATT_BLOCK), jnp.bfloat16)
    row_spec = pl.BlockSpec((None, ROW_TILE, d_grp), lambda b, i: (b, i, 0))
    fm_spec = pl.BlockSpec((None, n_sub, d_grp, ATT_BLOCK), lambda b, i: (b, i, 0, 0))
    return ([row_f32, row_f32, row_bf16, row_f32, fm_bf16, fm_bf16],
            [row_spec, row_spec, row_spec, row_spec, fm_spec, fm_spec])


def _projection_inputs(pre_g, w_in, layer):
    d = pre_g.shape[-1]
    return ([pre_g, w_in],
            [pl.BlockSpec((None, 1, d), lambda b, i: (layer, 0, 0)),
             pl.BlockSpec((None,) + w_in.shape[1:], lambda b, i: (layer, 0, 0))])


def _in_proj(h, pre_g, w_in, d_grp, layer):
    bsz, lp, d = h.shape
    out_shape, out_specs = _projection_outputs(bsz, lp, d_grp)
    operands, specs = _projection_inputs(pre_g, w_in, layer)
    return pl.pallas_call(
        _in_proj_kernel,
        grid=(bsz, lp // ROW_TILE),
        in_specs=[pl.BlockSpec((None, ROW_TILE, d), lambda b, i: (b, i, 0))] + specs,
        out_specs=out_specs,
        out_shape=out_shape,
        compiler_params=pltpu.CompilerParams(
            dimension_semantics=("parallel", "parallel"), vmem_limit_bytes=VMEM_LIMIT),
        name="in_proj",
    )(h, *operands)


def _conv_branch(cur_ref, halo_ref, cgate_ref, cw_ref, cb_ref, lg_ref, lb_ref, wp_ref, bp_ref,
                 xs_ref, y_ref):
    i = pl.program_id(1)
    halo = halo_ref[...]
    xs_ref[0, 0:CONV_HALO, :] = jnp.where(i > 0, halo, jnp.zeros_like(halo))
    xs_ref[0, CONV_HALO:, :] = cur_ref[...]
    span = ROW_TILE + CONV_HALO - SUBLANES
    for r in range(1, SUBLANES):
        xs_ref[r, 0:span, :] = xs_ref[0, r:r + span, :]

    lg = lg_ref[...]
    lb = lb_ref[...]
    first = CONV_HALO - (CONV_WIDTH - 1)
    n_lane_cols = cb_ref.shape[-1] // LANES
    for r0 in range(0, ROW_TILE, CONV_CHUNK):
        cols = []
        for lc in range(n_lane_cols):
            lanes = slice(lc * LANES, (lc + 1) * LANES)
            acc = jnp.broadcast_to(cb_ref[:, lanes], (CONV_CHUNK, LANES))
            for j in range(CONV_WIDTH):
                tiles, r = divmod(first + j, SUBLANES)
                rows = slice(r0 + tiles * SUBLANES, r0 + tiles * SUBLANES + CONV_CHUNK)
                acc = acc + cw_ref[j:j + 1, lanes] * xs_ref[r, rows, lanes]
            cols.append(acc)
        acc = jnp.concatenate(cols, axis=-1)
        mu = jnp.mean(acc, axis=-1, keepdims=True)
        cen = acc - mu
        var = jnp.mean(cen * cen, axis=-1, keepdims=True)
        y = cen * lax.rsqrt(var + LN_EPS) * lg + lb
        y_ref[r0:r0 + CONV_CHUNK, :] = (y * _sigmoid(y)).astype(jnp.bfloat16)

    p = _dot(y_ref[...], wp_ref[...]) + bp_ref[...]
    return (p * cgate_ref[...]).astype(jnp.bfloat16)


def _store_final_rows(h_new, out_hbm, buf_ref, sem, first_row, n_rows):
    b, i = pl.program_id(0), pl.program_id(1)
    n_b, n_t = pl.num_programs(0), pl.num_programs(1)
    rows = buf_ref.shape[0]
    last_rows = first_row + n_rows - (n_t - 1) * rows

    def copy(kind, batch, tile):
        if kind == "first":
            src, dst0, n = buf_ref.at[pl.ds(first_row, rows - first_row)], 0, rows - first_row
        elif kind == "last":
            src, dst0, n = buf_ref.at[pl.ds(0, last_rows)], n_rows - last_rows, last_rows
        else:
            src, dst0, n = buf_ref, pl.multiple_of(tile * rows - first_row, SUBLANES), rows
        return pltpu.make_async_copy(src, out_hbm.at[batch, pl.ds(dst0, n)], sem.at[0])

    @pl.when(jnp.logical_and(i == 0, b > 0))
    def _():
        copy("last", b - 1, n_t - 1).wait()

    @pl.when(i == 1)
    def _():
        copy("first", b, 0).wait()

    @pl.when(i >= 2)
    def _():
        copy("mid", b, i - 1).wait()

    buf_ref[...] = h_new

    @pl.when(i == 0)
    def _():
        copy("first", b, 0).start()

    @pl.when(jnp.logical_and(i > 0, i < n_t - 1))
    def _():
        copy("mid", b, i).start()

    @pl.when(i == n_t - 1)
    def _():
        copy("last", b, i).start()

    @pl.when(jnp.logical_and(i == n_t - 1, b == n_b - 1))
    def _():
        copy("last", b, i).wait()


def _mix_out_kernel(*refs, has_next, final_rows):
    (cur_ref, halo_ref, cgate_ref, cw_ref, cb_ref, lg_ref, lb_ref, wp_ref, bp_ref,
     s_ref, h_ref, wo_ref, pg_ref) = refs[:13]
    rest = refs[13:]
    next_in, rest = (rest[:2], rest[2:]) if has_next else ((), rest)
    o_ref = rest[0]
    next_out = rest[1:7] if has_next else ()
    xs_ref, y_ref = rest[-2:] if final_rows is None else rest[-4:-2]

    c = _conv_branch(cur_ref, halo_ref, cgate_ref, cw_ref, cb_ref, lg_ref, lb_ref, wp_ref, bp_ref,
                     xs_ref, y_ref)
    d_conv = c.shape[-1]
    mixed = _dot(c, wo_ref[0:d_conv, :]) + _dot(s_ref[...], wo_ref[d_conv:, :])
    ms = jnp.mean(mixed * mixed, axis=-1, keepdims=True)
    h_new = h_ref[...] + mixed * lax.rsqrt(ms + RMS_EPS) * pg_ref[...]
    if final_rows is None:
        o_ref[...] = h_new
    else:
        _store_final_rows(h_new, o_ref, *rest[-2:], *final_rows)
    if has_next:
        _project(h_new, *next_in, *next_out)


def _mix_out(glu, cgate, s, h, conv_w, conv_b, ln_g, ln_b, w_pw2, b_pw2, w_out, post_g, layer,
             next_proj=None, final_rows=None):
    assert (next_proj is None) != (final_rows is None)
    bsz, lp, dc = glu.shape
    d = h.shape[-1]
    halo_per_tile = ROW_TILE // CONV_HALO
    vec_spec = pl.BlockSpec((None, 1, dc), lambda b, i: (layer, 0, 0))
    row_spec = pl.BlockSpec((None, ROW_TILE, dc), lambda b, i: (b, i, 0))
    h_spec = pl.BlockSpec((None, ROW_TILE, d), lambda b, i: (b, i, 0))
    operands = [glu, glu, cgate, conv_w, conv_b, ln_g, ln_b, w_pw2, b_pw2, s, h, w_out, post_g]
    in_specs = [
        row_spec,
        pl.BlockSpec((None, CONV_HALO, dc),
                     lambda b, i: (b, jnp.maximum(i * halo_per_tile - 1, 0), 0)),
        row_spec,
        pl.BlockSpec((None, CONV_WIDTH, dc), lambda b, i: (layer, 0, 0)),
        vec_spec, vec_spec, vec_spec,
        pl.BlockSpec((None, dc, dc), lambda b, i: (layer, 0, 0)),
        vec_spec,
        pl.BlockSpec((None, ROW_TILE, s.shape[-1]), lambda b, i: (b, i, 0)),
        h_spec,
        pl.BlockSpec((None,) + w_out.shape[1:], lambda b, i: (layer, 0, 0)),
        pl.BlockSpec((None, 1, d), lambda b, i: (layer, 0, 0)),
    ]
    scratch_shapes = [pltpu.VMEM((SUBLANES, ROW_TILE + CONV_HALO, dc), jnp.float32),
                      pltpu.VMEM((ROW_TILE, dc), jnp.bfloat16)]
    if next_proj is not None:
        out_shape = [jax.ShapeDtypeStruct((bsz, lp, d), jnp.float32)]
        out_specs = [h_spec]
        more_operands, more_specs = _projection_inputs(*next_proj, layer + 1)
        operands += more_operands
        in_specs += more_specs
        proj_shape, proj_specs = _projection_outputs(bsz, lp, dc)
        out_shape += proj_shape
        out_specs += proj_specs
        semantics = ("parallel", "parallel")
    else:
        first_row, n_rows = final_rows
        n_tiles = lp // ROW_TILE
        assert n_tiles >= 3 and first_row % SUBLANES == 0 and 0 < first_row < ROW_TILE
        assert 0 < first_row + n_rows - (n_tiles - 1) * ROW_TILE <= ROW_TILE
        out_shape = [jax.ShapeDtypeStruct((bsz, n_rows, d), jnp.float32)]
        out_specs = [pl.BlockSpec(memory_space=pl.ANY)]
        scratch_shapes += [pltpu.VMEM((ROW_TILE, d), jnp.float32), pltpu.SemaphoreType.DMA((1,))]
        semantics = ("arbitrary", "arbitrary")
    return pl.pallas_call(
        functools.partial(_mix_out_kernel, has_next=next_proj is not None, final_rows=final_rows),
        grid=(bsz, lp // ROW_TILE),
        in_specs=in_specs,
        out_specs=out_specs,
        out_shape=out_shape,
        scratch_shapes=scratch_shapes,
        compiler_params=pltpu.CompilerParams(
            dimension_semantics=semantics, vmem_limit_bytes=VMEM_LIMIT_MIX_OUT),
        name="mix_out",
    )(*operands)


def _log_gates(z):
    log_go = jnp.minimum(z, 0.0) - jnp.log(1.0 + jnp.exp(jnp.minimum(z, -z)))
    return log_go, log_go - z


def _split_bf16(x):
    hi = x.astype(jnp.bfloat16)
    lo = (x - hi.astype(jnp.float32)).astype(jnp.bfloat16)
    return hi, lo


def _add_rows8(x, c8):
    r, n = x.shape
    return (x.reshape(r // 8, 8, n) + c8[None]).reshape(r, n)


def _block_start(j):
    start = j * ATT_BLOCK
    return start if isinstance(start, int) else pl.multiple_of(start, ATT_BLOCK)


def _sb_attn_kernel(qt_ref, k_ref, vt_ref, sgate_ref, tri_ref, s_ref, acc_ref, carry_ref, *, n_blk):
    blk = ATT_BLOCK
    full = FIRST_STEP_BLOCKS
    cut = blk - OLDEST_ROWS
    tri = tri_ref[...]
    row = lax.broadcasted_iota(jnp.int32, (LANES, blk), 0)
    head_rows = (row < HEAD_DIM, row >= HEAD_DIM)
    key_row = lax.broadcasted_iota(jnp.int32, (blk, blk), 0)
    diag_visible = key_row < lax.broadcasted_iota(jnp.int32, (blk, blk), 1)
    head = lambda hh: slice(hh * HEAD_DIM, (hh + 1) * HEAD_DIM)
    lane_blk = lambda m: slice(m * blk, (m + 1) * blk)
    no_rows = jnp.zeros((cut, blk), jnp.bfloat16)

    def load_keys(j):
        return k_ref[pl.ds(_block_start(j), blk), :], vt_ref[j]

    def head_queries(i):
        qt = qt_ref[i]
        zero = jnp.zeros_like(qt)
        return [jnp.where(m, qt, zero) for m in head_rows]

    def write_out(u, i):
        rows = pl.ds(_block_start(i), blk)
        out = acc_ref[u].T
        s_ref[rows, :] = (out * sgate_ref[rows, :]).astype(jnp.bfloat16)

    def older_rows(u, i, j, n_rows):
        q_heads = head_queries(i)
        kb, vb = load_keys(j)
        wanted = key_row < n_rows
        for hh in range(2):
            z = _dot(kb, q_heads[hh])
            log_go, log_stay = _log_gates(z)
            log_stay = jnp.where(wanted, log_stay, 0.0)
            sums = _dot(tri, jnp.concatenate(_split_bf16(log_stay), axis=0))
            carry = carry_ref[u, hh]
            a = jnp.where(wanted, jnp.exp(_add_rows8(log_go + sums[0:blk], carry)), 0.0)
            acc_ref[u, head(hh), :] += _dot(vb[head(hh), :], a.astype(jnp.bfloat16))
            carry_ref[u, hh] = carry + sums[blk:blk + 8]

    def finish(base, n_q, live, stage):
        for u in range(n_q):
            i = base + u
            oldest = i - (full - 1)
            if isinstance(oldest, int) and oldest < 0:
                continue
            if stage == 0:
                @pl.when(jnp.logical_and(live[u] > 0, oldest >= 0))
                def _():
                    def cond(c):
                        j, go = c
                        return jnp.logical_and(j >= 0, go > 0)

                    def body(c):
                        j, _ = c
                        older_rows(u, i, j, jnp.where(j == oldest, cut, blk))
                        go = (jnp.max(carry_ref[u]) >= LOG_F32_UNDERFLOW).astype(jnp.int32)
                        return j - 1, go

                    lax.while_loop(cond, body, (jnp.asarray(oldest, jnp.int32), jnp.int32(1)))
        if stage == 1:
            for u in range(n_q):
                write_out(u, base + u)

    def q_blocks(base, depths, pending=None):
        n_q = len(depths)
        if pending is not None:
            finish(pending[0], n_q, pending[1], stage=0)
            finish(pending[0], n_q, pending[1], stage=1)
        q_heads = [head_queries(base + u) for u in range(n_q)]
        chains = [(u, n, hh) for u in range(n_q) for n in range(depths[u]) for hh in range(2)]
        partial = lambda c: c[1] == full - 1
        offsets = sorted({u - n for u, n, _ in chains})
        keys = {d: load_keys(base + d) for d in offsets}

        z = {}
        for d in offsets:
            users = [c for c in chains if c[0] - c[1] == d]
            wide = _dot(keys[d][0], jnp.concatenate([q_heads[u][hh] for u, _, hh in users], axis=1))
            for m, c in enumerate(users):
                z[c] = wide[cut:, lane_blk(m)] if partial(c) else wide[:, lane_blk(m)]

        log_go, log_stay, halves = {}, {}, []
        for c in chains:
            log_go[c], stay = _log_gates(z[c])
            if c[1] == 0:
                stay = jnp.where(diag_visible, stay, 0.0)
            log_stay[c] = stay
            hi, lo = _split_bf16(stay)
            halves.append(jnp.concatenate([no_rows, hi, no_rows, lo] if partial(c) else [hi, lo],
                                          axis=0))

        wide = _dot(tri, jnp.concatenate(halves, axis=1))
        sums = {c: wide[:, lane_blk(m)] for m, c in enumerate(chains)}

        last = {}
        for u in range(n_q):
            for hh in range(2):
                carry = None
                weights = []
                for n in range(depths[u]):
                    c = (u, n, hh)
                    arg = log_go[c] + (sums[c][cut:blk] if partial(c) else sums[c][0:blk])
                    if carry is not None:
                        arg = _add_rows8(arg, carry)
                    e = jnp.exp(arg)
                    if n == 0:
                        e = jnp.where(diag_visible, e, 0.0)
                    if partial(c):
                        weights.append(no_rows)
                    weights.append(e.astype(jnp.bfloat16))
                    total = sums[c][blk:blk + 8]
                    carry = total if carry is None else carry + total
                values = [keys[u - n][1][head(hh), :] for n in range(depths[u])]
                acc_ref[u, head(hh), :] = _dot(jnp.concatenate(values, axis=1),
                                               jnp.concatenate(weights, axis=0))
                carry_ref[u, hh] = carry
                last[u, hh] = carry

        return tuple(
            (jnp.max(jnp.maximum(last[u, 0], last[u, 1])) >= LOG_F32_UNDERFLOW).astype(jnp.int32)
            for u in range(n_q))

    unroll = ATT_UNROLL
    lead = full - 1
    lead += (n_blk - lead) % unroll
    if lead < unroll:
        lead += unroll
    assert lead <= n_blk and (n_blk - lead) % unroll == 0
    starts = ([0] if lead % unroll else []) + list(range(lead % unroll, lead, unroll))
    for g, base in enumerate(starts):
        end = starts[g + 1] if g + 1 < len(starts) else lead
        live = q_blocks(base, [min(i + 1, full) for i in range(base, end)])
        if end < lead:
            finish(base, end - base, live, stage=0)
            finish(base, end - base, live, stage=1)

    def loop_body(t, live):
        base = lead + t * unroll
        return q_blocks(base, [full] * unroll, pending=(base - unroll, live))

    live = lax.fori_loop(0, (n_blk - lead) // unroll, loop_body, live)
    finish(n_blk - unroll, unroll, live, stage=0)
    finish(n_blk - unroll, unroll, live, stage=1)


def _sb_attn(qt, k, vt, sgate, tri):
    bsz, n_blk, d_sb, blk = qt.shape
    lp = k.shape[1]
    n_pairs = d_sb // LANES
    return pl.pallas_call(
        functools.partial(_sb_attn_kernel, n_blk=n_blk),
        grid=(bsz, n_pairs),
        in_specs=[
            pl.BlockSpec((None, n_blk, LANES, blk), lambda b, p: (b, 0, p, 0)),
            pl.BlockSpec((None, lp, LANES), lambda b, p: (b, 0, p)),
            pl.BlockSpec((None, n_blk, LANES, blk), lambda b, p: (b, 0, p, 0)),
            pl.BlockSpec((None, lp, LANES), lambda b, p: (b, 0, p)),
            pl.BlockSpec(tri.shape, lambda b, p: (0, 0)),
        ],
        out_specs=pl.BlockSpec((None, lp, LANES), lambda b, p: (b, 0, p)),
        out_shape=jax.ShapeDtypeStruct((bsz, lp, d_sb), jnp.bfloat16),
        scratch_shapes=[pltpu.VMEM((ATT_UNROLL, LANES, blk), jnp.float32),
                        pltpu.VMEM((ATT_UNROLL, 2, 8, blk), jnp.float32)],
        compiler_params=pltpu.CompilerParams(
            dimension_semantics=("parallel", "parallel"), vmem_limit_bytes=VMEM_LIMIT),
        name="sb_attn",
    )(qt, k, vt, sgate, tri)


def _suffix_sum_matrix():
    r = lax.broadcasted_iota(jnp.int32, (TRI_ROWS, 2 * ATT_BLOCK), 0)
    c = lax.broadcasted_iota(jnp.int32, (TRI_ROWS, 2 * ATT_BLOCK), 1) % ATT_BLOCK
    return jnp.where((r >= ATT_BLOCK) | (c > r), 1.0, 0.0).astype(jnp.bfloat16)


def kernel(x, meta_tokens, pre_norm_g, post_norm_g, w_in, conv_w, conv_b, conv_ln_g, conv_ln_b,
           w_pw2, b_pw2, w_out):
    bsz, seq, d = x.shape
    depth = w_in.shape[0]
    d_conv = conv_w.shape[-1]
    d_sb = N_HEADS * HEAD_DIM
    assert w_in.shape[-1] == 3 * d_conv + 4 * d_sb and d_conv == d_sb
    assert ROW_TILE % ATT_BLOCK == 0 and ROW_TILE % CONV_CHUNK == 0
    assert CONV_HALO % SUBLANES == 0 and CONV_HALO >= CONV_WIDTH - 1 and ROW_TILE % CONV_HALO == 0

    length = N_META + seq
    lp = -(-length // ROW_TILE) * ROW_TILE
    meta = jnp.broadcast_to(meta_tokens.astype(x.dtype)[None], (bsz, N_META, d))
    h = jnp.concatenate([meta, x, jnp.zeros((bsz, lp - length, d), x.dtype)], axis=1)

    bf16 = jnp.bfloat16
    w_in_b = w_in.astype(bf16)
    w_pw2_b = w_pw2.astype(bf16)
    w_out_b = w_out.astype(bf16)
    vec = lambda a: a.reshape(depth, 1, a.shape[-1])
    pre_g, post_g = vec(pre_norm_g), vec(post_norm_g)
    conv_b3, ln_g3, ln_b3, b_pw3 = vec(conv_b), vec(conv_ln_g), vec(conv_ln_b), vec(b_pw2)
    tri = _suffix_sum_matrix()

    glu, cgate, k, sgate, qt, vt = _in_proj(h, pre_g, w_in_b, d_conv, 0)
    for layer in range(depth):
        s = _sb_attn(qt, k, vt, sgate, tri)
        last = layer + 1 == depth
        outs = _mix_out(glu, cgate, s, h, conv_w, conv_b3, ln_g3, ln_b3, w_pw2_b, b_pw3,
                        w_out_b, post_g, layer,
                        next_proj=None if last else (pre_g, w_in_b),
                        final_rows=(N_META, seq) if last else None)
        h = outs[0]
        if not last:
            glu, cgate, k, sgate, qt, vt = outs[1:]
    return h
```

```python
import functools

import jax
import jax.numpy as jnp
from jax import lax
from jax.experimental import pallas as pl
from jax.experimental.pallas import tpu as pltpu

N_META = 16
N_HEADS = 8
HEAD_DIM = 64
CONV_WIDTH = 31
RMS_EPS = 1e-6
LN_EPS = 1e-5

LANES = 128
SUBLANES = 8
ROW_TILE = 384
ATT_BLOCK = 128
FIRST_STEP_BLOCKS = 3
OLDEST_ROWS = 32
ATT_UNROLL = 6
CONV_HALO = 32
CONV_CHUNK = 64
TRI_ROWS = ATT_BLOCK + 16
LOG_F32_UNDERFLOW = -104.0
VMEM_LIMIT = 48 * 1024 * 1024
VMEM_LIMIT_MIX_OUT = 58 * 1024 * 1024


def _sigmoid(x):
    return 1.0 / (1.0 + jnp.exp(-x))


def _dot(a, b):
    return jnp.dot(a, b, preferred_element_type=jnp.float32)


def _project(x, g_ref, w_ref, glu_ref, cgate_ref, k_ref, sgate_ref, qt_ref, vt_ref):
    d_grp = glu_ref.shape[-1]
    ms = jnp.mean(x * x, axis=-1, keepdims=True)
    u = (x * lax.rsqrt(ms + RMS_EPS) * g_ref[...]).astype(jnp.bfloat16)
    group = lambda m: _dot(u, w_ref[:, m * d_grp:(m + 1) * d_grp])

    glu_ref[...] = group(0) * _sigmoid(group(1))
    cg = group(2)
    cgate_ref[...] = cg * _sigmoid(cg)
    k_ref[...] = group(4).astype(jnp.bfloat16)
    sg = group(6)
    sgate_ref[...] = sg * _sigmoid(sg)

    qt = (group(3) * (HEAD_DIM ** -0.5)).T.astype(jnp.bfloat16)
    vt = group(5).T.astype(jnp.bfloat16)
    for c in range(qt_ref.shape[0]):
        qt_ref[c] = qt[:, c * ATT_BLOCK:(c + 1) * ATT_BLOCK]
        vt_ref[c] = vt[:, c * ATT_BLOCK:(c + 1) * ATT_BLOCK]


def _seq_rows(kind, tile, rows, first_row, n_rows, n_tiles):
    last_rows = first_row + n_rows - (n_tiles - 1) * rows
    if kind == "first":
        return first_row, 0, rows - first_row
    if kind == "last":
        return 0, n_rows - last_rows, last_rows
    return 0, pl.multiple_of(tile * rows - first_row, SUBLANES), rows


def _by_tile_kind(i, n_tiles, fn):
    pl.when(i == 0)(lambda: fn("first"))
    pl.when(jnp.logical_and(i > 0, i < n_tiles - 1))(lambda: fn("mid"))
    pl.when(i == n_tiles - 1)(lambda: fn("last"))


def _in_proj_first_kernel(x_hbm, meta_ref, g_ref, w_ref, h_ref, *rest, n_rows):
    out_refs, (xbuf_ref, sem) = rest[:-2], rest[-2:]
    b, i = pl.program_id(0), pl.program_id(1)
    n_b, n_t = pl.num_programs(0), pl.num_programs(1)
    first_row, rows = meta_ref.shape[0], h_ref.shape[0]
    step = b * n_t + i
    slot = step % 2

    def fetch(kind, batch, tile, to_slot):
        r0, s0, n = _seq_rows(kind, tile, rows, first_row, n_rows, n_t)
        return pltpu.make_async_copy(x_hbm.at[batch, pl.ds(s0, n)],
                                     xbuf_ref.at[to_slot, pl.ds(r0, n)], sem.at[to_slot])

    @pl.when(step == 0)
    def _():
        fetch("first", 0, 0, 0).start()

    @pl.when(step + 1 < n_b * n_t)
    def _():
        nxt = jnp.where(i + 1 < n_t, i + 1, 0)
        nxt_b = jnp.where(i + 1 < n_t, b, b + 1)
        _by_tile_kind(nxt, n_t, lambda kind: fetch(kind, nxt_b, nxt, 1 - slot).start())

    _by_tile_kind(i, n_t, lambda kind: fetch(kind, b, i, slot).wait())

    @pl.when(i == 0)
    def _():
        xbuf_ref[slot, 0:first_row, :] = meta_ref[...]

    @pl.when(i == n_t - 1)
    def _():
        valid = _seq_rows("last", i, rows, first_row, n_rows, n_t)[2]
        xbuf_ref[slot, valid:, :] = jnp.zeros((rows - valid, xbuf_ref.shape[-1]), xbuf_ref.dtype)

    x = xbuf_ref[slot]
    h_ref[...] = x
    _project(x, g_ref, w_ref, *out_refs)


def _projection_outputs(bsz, lp, d_grp):
    n_sub = ROW_TILE // ATT_BLOCK
    row_f32 = jax.ShapeDtypeStruct((bsz, lp, d_grp), jnp.float32)
    row_bf16 = jax.ShapeDtypeStruct((bsz, lp, d_grp), jnp.bfloat16)
    fm_bf16 = jax.ShapeDtypeStruct((bsz, lp // ATT_BLOCK, d_grp, ATT_BLOCK), jnp.bfloat16)
    row_spec = pl.BlockSpec((None, ROW_TILE, d_grp), lambda b, i: (b, i, 0))
    fm_spec = pl.BlockSpec((None, n_sub, d_grp, ATT_BLOCK), lambda b, i: (b, i, 0, 0))
    return ([row_f32, row_f32, row_bf16, row_f32, fm_bf16, fm_bf16],
            [row_spec, row_spec, row_spec, row_spec, fm_spec, fm_spec])


def _projection_inputs(pre_g, w_in, layer):
    d = pre_g.shape[-1]
    return ([pre_g, w_in],
            [pl.BlockSpec((None, 1, d), lambda b, i: (layer, 0, 0)),
             pl.BlockSpec((None,) + w_in.shape[1:], lambda b, i: (layer, 0, 0))])


def _in_proj_first(x, meta, lp, pre_g, w_in, d_grp):
    bsz, seq, d = x.shape
    n_tiles = lp // ROW_TILE
    first_row = meta.shape[0]
    assert n_tiles >= 3 and first_row % SUBLANES == 0 and 0 < first_row < ROW_TILE
    assert 0 < first_row + seq - (n_tiles - 1) * ROW_TILE <= ROW_TILE
    out_shape, out_specs = _projection_outputs(bsz, lp, d_grp)
    operands, specs = _projection_inputs(pre_g, w_in, 0)
    h_spec = pl.BlockSpec((None, ROW_TILE, d), lambda b, i: (b, i, 0))
    return pl.pallas_call(
        functools.partial(_in_proj_first_kernel, n_rows=seq),
        grid=(bsz, n_tiles),
        in_specs=[pl.BlockSpec(memory_space=pl.ANY),
                  pl.BlockSpec(meta.shape, lambda b, i: (0, 0))] + specs,
        out_specs=[h_spec] + out_specs,
        out_shape=[jax.ShapeDtypeStruct((bsz, lp, d), x.dtype)] + out_shape,
        scratch_shapes=[pltpu.VMEM((2, ROW_TILE, d), x.dtype), pltpu.SemaphoreType.DMA((2,))],
        compiler_params=pltpu.CompilerParams(
            dimension_semantics=("arbitrary", "arbitrary"),
            vmem_limit_bytes=VMEM_LIMIT),
        name="in_proj",
    )(x, meta, *operands)


def _conv_branch(cur_ref, halo_ref, cgate_ref, cw_ref, cb_ref, lg_ref, lb_ref, wp_ref, bp_ref,
                 xs_ref, y_ref):
    i = pl.program_id(1)
    halo = halo_ref[...]
    xs_ref[0, 0:CONV_HALO, :] = jnp.where(i > 0, halo, jnp.zeros_like(halo))
    xs_ref[0, CONV_HALO:, :] = cur_ref[...]
    span = ROW_TILE + CONV_HALO - SUBLANES
    for r in range(1, SUBLANES):
        xs_ref[r, 0:span, :] = xs_ref[0, r:r + span, :]

    lg = lg_ref[...]
    lb = lb_ref[...]
    first = CONV_HALO - (CONV_WIDTH - 1)
    n_lane_cols = cb_ref.shape[-1] // LANES
    for r0 in range(0, ROW_TILE, CONV_CHUNK):
        cols = []
        for lc in range(n_lane_cols):
            lanes = slice(lc * LANES, (lc + 1) * LANES)
            acc = jnp.broadcast_to(cb_ref[:, lanes], (CONV_CHUNK, LANES))
            for j in range(CONV_WIDTH):
                tiles, r = divmod(first + j, SUBLANES)
                rows = slice(r0 + tiles * SUBLANES, r0 + tiles * SUBLANES + CONV_CHUNK)
                acc = acc + cw_ref[j:j + 1, lanes] * xs_ref[r, rows, lanes]
            cols.append(acc)
        acc = jnp.concatenate(cols, axis=-1)
        mu = jnp.mean(acc, axis=-1, keepdims=True)
        cen = acc - mu
        var = jnp.mean(cen * cen, axis=-1, keepdims=True)
        y = cen * lax.rsqrt(var + LN_EPS) * lg + lb
        y_ref[r0:r0 + CONV_CHUNK, :] = (y * _sigmoid(y)).astype(jnp.bfloat16)

    p = _dot(y_ref[...], wp_ref[...]) + bp_ref[...]
    return (p * cgate_ref[...]).astype(jnp.bfloat16)


def _store_final_rows(h_new, out_hbm, buf_ref, sem, first_row, n_rows):
    b, i = pl.program_id(0), pl.program_id(1)
    n_b, n_t = pl.num_programs(0), pl.num_programs(1)
    rows = buf_ref.shape[0]

    def copy(kind, batch, tile):
        r0, s0, n = _seq_rows(kind, tile, rows, first_row, n_rows, n_t)
        return pltpu.make_async_copy(buf_ref.at[pl.ds(r0, n)], out_hbm.at[batch, pl.ds(s0, n)],
                                     sem.at[0])

    @pl.when(jnp.logical_and(i == 0, b > 0))
    def _():
        copy("last", b - 1, n_t - 1).wait()

    @pl.when(i == 1)
    def _():
        copy("first", b, 0).wait()

    @pl.when(i >= 2)
    def _():
        copy("mid", b, i - 1).wait()

    buf_ref[...] = h_new
    _by_tile_kind(i, n_t, lambda kind: copy(kind, b, i).start())

    @pl.when(jnp.logical_and(i == n_t - 1, b == n_b - 1))
    def _():
        copy("last", b, i).wait()


def _mix_out_kernel(*refs, has_next, final_rows):
    (cur_ref, halo_ref, cgate_ref, cw_ref, cb_ref, lg_ref, lb_ref, wp_ref, bp_ref,
     s_ref, h_ref, wo_ref, pg_ref) = refs[:13]
    rest = refs[13:]
    next_in, rest = (rest[:2], rest[2:]) if has_next else ((), rest)
    o_ref = rest[0]
    next_out = rest[1:7] if has_next else ()
    xs_ref, y_ref = rest[-2:] if final_rows is None else rest[-4:-2]

    c = _conv_branch(cur_ref, halo_ref, cgate_ref, cw_ref, cb_ref, lg_ref, lb_ref, wp_ref, bp_ref,
                     xs_ref, y_ref)
    d_conv = c.shape[-1]
    mixed = _dot(c, wo_ref[0:d_conv, :]) + _dot(s_ref[...], wo_ref[d_conv:, :])
    ms = jnp.mean(mixed * mixed, axis=-1, keepdims=True)
    h_new = h_ref[...] + mixed * lax.rsqrt(ms + RMS_EPS) * pg_ref[...]
    if final_rows is None:
        o_ref[...] = h_new
    else:
        _store_final_rows(h_new, o_ref, *rest[-2:], *final_rows)
    if has_next:
        _project(h_new, *next_in, *next_out)


def _mix_out(glu, cgate, s, h, conv_w, conv_b, ln_g, ln_b, w_pw2, b_pw2, w_out, post_g, layer,
             next_proj=None, final_rows=None):
    assert (next_proj is None) != (final_rows is None)
    bsz, lp, dc = glu.shape
    d = h.shape[-1]
    halo_per_tile = ROW_TILE // CONV_HALO
    vec_spec = pl.BlockSpec((None, 1, dc), lambda b, i: (layer, 0, 0))
    row_spec = pl.BlockSpec((None, ROW_TILE, dc), lambda b, i: (b, i, 0))
    h_spec = pl.BlockSpec((None, ROW_TILE, d), lambda b, i: (b, i, 0))
    operands = [glu, glu, cgate, conv_w, conv_b, ln_g, ln_b, w_pw2, b_pw2, s, h, w_out, post_g]
    in_specs = [
        row_spec,
        pl.BlockSpec((None, CONV_HALO, dc),
                     lambda b, i: (b, jnp.maximum(i * halo_per_tile - 1, 0), 0)),
        row_spec,
        pl.BlockSpec((None, CONV_WIDTH, dc), lambda b, i: (layer, 0, 0)),
        vec_spec, vec_spec, vec_spec,
        pl.BlockSpec((None, dc, dc), lambda b, i: (layer, 0, 0)),
        vec_spec,
        pl.BlockSpec((None, ROW_TILE, s.shape[-1]), lambda b, i: (b, i, 0)),
        h_spec,
        pl.BlockSpec((None,) + w_out.shape[1:], lambda b, i: (layer, 0, 0)),
        pl.BlockSpec((None, 1, d), lambda b, i: (layer, 0, 0)),
    ]
    scratch_shapes = [pltpu.VMEM((SUBLANES, ROW_TILE + CONV_HALO, dc), jnp.float32),
                      pltpu.VMEM((ROW_TILE, dc), jnp.bfloat16)]
    if next_proj is not None:
        out_shape = [jax.ShapeDtypeStruct((bsz, lp, d), jnp.float32)]
        out_specs = [h_spec]
        more_operands, more_specs = _projection_inputs(*next_proj, layer + 1)
        operands += more_operands
        in_specs += more_specs
        proj_shape, proj_specs = _projection_outputs(bsz, lp, dc)
        out_shape += proj_shape
        out_specs += proj_specs
        semantics = ("parallel", "parallel")
    else:
        first_row, n_rows = final_rows
        n_tiles = lp // ROW_TILE
        assert n_tiles >= 3 and first_row % SUBLANES == 0 and 0 < first_row < ROW_TILE
        assert 0 < first_row + n_rows - (n_tiles - 1) * ROW_TILE <= ROW_TILE
        out_shape = [jax.ShapeDtypeStruct((bsz, n_rows, d), jnp.float32)]
        out_specs = [pl.BlockSpec(memory_space=pl.ANY)]
        scratch_shapes += [pltpu.VMEM((ROW_TILE, d), jnp.float32), pltpu.SemaphoreType.DMA((1,))]
        semantics = ("arbitrary", "arbitrary")
    return pl.pallas_call(
        functools.partial(_mix_out_kernel, has_next=next_proj is not None, final_rows=final_rows),
        grid=(bsz, lp // ROW_TILE),
        in_specs=in_specs,
        out_specs=out_specs,
        out_shape=out_shape,
        scratch_shapes=scratch_shapes,
        compiler_params=pltpu.CompilerParams(
            dimension_semantics=semantics, vmem_limit_bytes=VMEM_LIMIT_MIX_OUT),
        name="mix_out",
    )(*operands)


def _log_gates(z):
    log_go = jnp.minimum(z, 0.0) - jnp.log(1.0 + jnp.exp(jnp.minimum(z, -z)))
    return log_go, log_go - z


def _split_bf16(x):
    hi = x.astype(jnp.bfloat16)
    lo = (x - hi.astype(jnp.float32)).astype(jnp.bfloat16)
    return hi, lo


def _add_rows8(x, c8):
    r, n = x.shape
    return (x.reshape(r // 8, 8, n) + c8[None]).reshape(r, n)


def _block_start(j):
    start = j * ATT_BLOCK
    return start if isinstance(start, int) else pl.multiple_of(start, ATT_BLOCK)


def _sb_attn_kernel(qt_ref, k_ref, vt_ref, sgate_ref, tri_ref, s_ref, acc_ref, carry_ref, *, n_blk):
    blk = ATT_BLOCK
    full = FIRST_STEP_BLOCKS
    cut = blk - OLDEST_ROWS
    tri = tri_ref[...]
    row = lax.broadcasted_iota(jnp.int32, (LANES, blk), 0)
    head_rows = (row < HEAD_DIM, row >= HEAD_DIM)
    key_row = lax.broadcasted_iota(jnp.int32, (blk, blk), 0)
    diag_visible = key_row < lax.broadcasted_iota(jnp.int32, (blk, blk), 1)
    head = lambda hh: slice(hh * HEAD_DIM, (hh + 1) * HEAD_DIM)
    lane_blk = lambda m: slice(m * blk, (m + 1) * blk)
    no_rows = jnp.zeros((cut, blk), jnp.bfloat16)

    def load_keys(j):
        return k_ref[pl.ds(_block_start(j), blk), :], vt_ref[j]

    def head_queries(i):
        qt = qt_ref[i]
        zero = jnp.zeros_like(qt)
        return [jnp.where(m, qt, zero) for m in head_rows]

    def write_out(u, i):
        rows = pl.ds(_block_start(i), blk)
        out = acc_ref[u].T
        s_ref[rows, :] = (out * sgate_ref[rows, :]).astype(jnp.bfloat16)

    def older_rows(u, i, j, n_rows):
        q_heads = head_queries(i)
        kb, vb = load_keys(j)
        wanted = key_row < n_rows
        for hh in range(2):
            z = _dot(kb, q_heads[hh])
            log_go, log_stay = _log_gates(z)
            log_stay = jnp.where(wanted, log_stay, 0.0)
            sums = _dot(tri, jnp.concatenate(_split_bf16(log_stay), axis=0))
            carry = carry_ref[u, hh]
            a = jnp.where(wanted, jnp.exp(_add_rows8(log_go + sums[0:blk], carry)), 0.0)
            acc_ref[u, head(hh), :] += _dot(vb[head(hh), :], a.astype(jnp.bfloat16))
            carry_ref[u, hh] = carry + sums[blk:blk + 8]

    def finish(base, n_q, live, stage):
        for u in range(n_q):
            i = base + u
            oldest = i - (full - 1)
            if isinstance(oldest, int) and oldest < 0:
                continue
            if stage == 0:
                @pl.when(jnp.logical_and(live[u] > 0, oldest >= 0))
                def _():
                    def cond(c):
                        j, go = c
                        return jnp.logical_and(j >= 0, go > 0)

                    def body(c):
                        j, _ = c
                        older_rows(u, i, j, jnp.where(j == oldest, cut, blk))
                        go = (jnp.max(carry_ref[u]) >= LOG_F32_UNDERFLOW).astype(jnp.int32)
                        return j - 1, go

                    lax.while_loop(cond, body, (jnp.asarray(oldest, jnp.int32), jnp.int32(1)))
        if stage == 1:
            for u in range(n_q):
                write_out(u, base + u)

    def q_blocks(base, depths, pending=None):
        n_q = len(depths)
        if pending is not None:
            finish(pending[0], n_q, pending[1], stage=0)
            finish(pending[0], n_q, pending[1], stage=1)
        q_heads = [head_queries(base + u) for u in range(n_q)]
        chains = [(u, n, hh) for u in range(n_q) for n in range(depths[u]) for hh in range(2)]
        partial = lambda c: c[1] == full - 1
        offsets = sorted({u - n for u, n, _ in chains})
        keys = {d: load_keys(base + d) for d in offsets}

        z = {}
        for d in offsets:
            users = [c for c in chains if c[0] - c[1] == d]
            wide = _dot(keys[d][0], jnp.concatenate([q_heads[u][hh] for u, _, hh in users], axis=1))
            for m, c in enumerate(users):
                z[c] = wide[cut:, lane_blk(m)] if partial(c) else wide[:, lane_blk(m)]

        log_go, log_stay, halves = {}, {}, []
        for c in chains:
            log_go[c], stay = _log_gates(z[c])
            if c[1] == 0:
                stay = jnp.where(diag_visible, stay, 0.0)
            log_stay[c] = stay
            hi, lo = _split_bf16(stay)
            halves.append(jnp.concatenate([no_rows, hi, no_rows, lo] if partial(c) else [hi, lo],
                                          axis=0))

        wide = _dot(tri, jnp.concatenate(halves, axis=1))
        sums = {c: wide[:, lane_blk(m)] for m, c in enumerate(chains)}

        last = {}
        for u in range(n_q):
            for hh in range(2):
                carry = None
                weights = []
                for n in range(depths[u]):
                    c = (u, n, hh)
                    arg = log_go[c] + (sums[c][cut:blk] if partial(c) else sums[c][0:blk])
                    if carry is not None:
                        arg = _add_rows8(arg, carry)
                    e = jnp.exp(arg)
                    if n == 0:
                        e = jnp.where(diag_visible, e, 0.0)
                    if partial(c):
                        weights.append(no_rows)
                    weights.append(e.astype(jnp.bfloat16))
                    total = sums[c][blk:blk + 8]
                    carry = total if carry is None else carry + total
                values = [keys[u - n][1][head(hh), :] for n in range(depths[u])]
                acc_ref[u, head(hh), :] = _dot(jnp.concatenate(values, axis=1),
                                               jnp.concatenate(weights, axis=0))
                carry_ref[u, hh] = carry
                last[u, hh] = carry

        return tuple(
            (jnp.max(jnp.maximum(last[u, 0], last[u, 1])) >= LOG_F32_UNDERFLOW).astype(jnp.int32)
            for u in range(n_q))

    unroll = ATT_UNROLL
    lead = full - 1
    lead += (n_blk - lead) % unroll
    if lead < unroll:
        lead += unroll
    assert lead <= n_blk and (n_blk - lead) % unroll == 0
    starts = ([0] if lead % unroll else []) + list(range(lead % unroll, lead, unroll))
    for g, base in enumerate(starts):
        end = starts[g + 1] if g + 1 < len(starts) else lead
        live = q_blocks(base, [min(i + 1, full) for i in range(base, end)])
        if end < lead:
            finish(base, end - base, live, stage=0)
            finish(base, end - base, live, stage=1)

    def loop_body(t, live):
        base = lead + t * unroll
        return q_blocks(base, [full] * unroll, pending=(base - unroll, live))

    live = lax.fori_loop(0, (n_blk - lead) // unroll, loop_body, live)
    finish(n_blk - unroll, unroll, live, stage=0)
    finish(n_blk - unroll, unroll, live, stage=1)


def _sb_attn(qt, k, vt, sgate, tri):
    bsz, n_blk, d_sb, blk = qt.shape
    lp = k.shape[1]
    n_pairs = d_sb // LANES
    return pl.pallas_call(
        functools.partial(_sb_attn_kernel, n_blk=n_blk),
        grid=(bsz, n_pairs),
        in_specs=[
            pl.BlockSpec((None, n_blk, LANES, blk), lambda b, p: (b, 0, p, 0)),
            pl.BlockSpec((None, lp, LANES), lambda b, p: (b, 0, p)),
            pl.BlockSpec((None, n_blk, LANES, blk), lambda b, p: (b, 0, p, 0)),
            pl.BlockSpec((None, lp, LANES), lambda b, p: (b, 0, p)),
            pl.BlockSpec(tri.shape, lambda b, p: (0, 0)),
        ],
        out_specs=pl.BlockSpec((None, lp, LANES), lambda b, p: (b, 0, p)),
        out_shape=jax.ShapeDtypeStruct((bsz, lp, d_sb), jnp.bfloat16),
        scratch_shapes=[pltpu.VMEM((ATT_UNROLL, LANES, blk), jnp.float32),
                        pltpu.VMEM((ATT_UNROLL, 2, 8, blk), jnp.float32)],
        compiler_params=pltpu.CompilerParams(
            dimension_semantics=("parallel", "parallel"), vmem_limit_bytes=VMEM_LIMIT),
        name="sb_attn",
    )(qt, k, vt, sgate, tri)


def _suffix_sum_matrix():
    r = lax.broadcasted_iota(jnp.int32, (TRI_ROWS, 2 * ATT_BLOCK), 0)
    c = lax.broadcasted_iota(jnp.int32, (TRI_ROWS, 2 * ATT_BLOCK), 1) % ATT_BLOCK
    return jnp.where((r >= ATT_BLOCK) | (c > r), 1.0, 0.0).astype(jnp.bfloat16)


def kernel(x, meta_tokens, pre_norm_g, post_norm_g, w_in, conv_w, conv_b, conv_ln_g, conv_ln_b,
           w_pw2, b_pw2, w_out):
    bsz, seq, d = x.shape
    depth = w_in.shape[0]
    d_conv = conv_w.shape[-1]
    d_sb = N_HEADS * HEAD_DIM
    assert w_in.shape[-1] == 3 * d_conv + 4 * d_sb and d_conv == d_sb
    assert ROW_TILE % ATT_BLOCK == 0 and ROW_TILE % CONV_CHUNK == 0
    assert CONV_HALO % SUBLANES == 0 and CONV_HALO >= CONV_WIDTH - 1 and ROW_TILE % CONV_HALO == 0

    length = N_META + seq
    lp = -(-length // ROW_TILE) * ROW_TILE

    bf16 = jnp.bfloat16
    w_in_b = w_in.astype(bf16)
    w_pw2_b = w_pw2.astype(bf16)
    w_out_b = w_out.astype(bf16)
    vec = lambda a: a.reshape(depth, 1, a.shape[-1])
    pre_g, post_g = vec(pre_norm_g), vec(post_norm_g)
    conv_b3, ln_g3, ln_b3, b_pw3 = vec(conv_b), vec(conv_ln_g), vec(conv_ln_b), vec(b_pw2)
    tri = _suffix_sum_matrix()

    h, glu, cgate, k, sgate, qt, vt = _in_proj_first(
        x, meta_tokens.astype(x.dtype), lp, pre_g, w_in_b, d_conv)
    for layer in range(depth):
        s = _sb_attn(qt, k, vt, sgate, tri)
        last = layer + 1 == depth
        outs = _mix_out(glu, cgate, s, h, conv_w, conv_b3, ln_g3, ln_b3, w_pw2_b, b_pw3,
                        w_out_b, post_g, layer,
                        next_proj=None if last else (pre_g, w_in_b),
                        final_rows=(N_META, seq) if last else None)
        h = outs[0]
        if not last:
            glu, cgate, k, sgate, qt, vt = outs[1:]
    return h
```

```python
import functools

import jax
import jax.numpy as jnp
from jax import lax
from jax.experimental import pallas as pl
from jax.experimental.pallas import tpu as pltpu

N_META = 16
N_HEADS = 8
HEAD_DIM = 64
CONV_WIDTH = 31
RMS_EPS = 1e-6
LN_EPS = 1e-5

LANES = 128
SUBLANES = 8
ROW_TILE = 384
ATT_BLOCK = 128
FIRST_STEP_BLOCKS = 3
OLDEST_ROWS = 32
ATT_UNROLL = 6
CONV_HALO = 32
CONV_CHUNK = 64
TIE_BACK = 2
TRI_ROWS = ATT_BLOCK + 16
LOG_F32_UNDERFLOW = -104.0
VMEM_LIMIT = 48 * 1024 * 1024
VMEM_LIMIT_MIX_OUT = 58 * 1024 * 1024


def _sigmoid(x):
    return 1.0 / (1.0 + jnp.exp(-x))


def _dot(a, b):
    return jnp.dot(a, b, preferred_element_type=jnp.float32)


def _project(x, g_ref, w_ref, glu_ref, cgate_ref, k_ref, sgate_ref, qt_ref, vt_ref, matmul=None):
    matmul = matmul or _dot
    d_grp = glu_ref.shape[-1]
    ms = jnp.mean(x * x, axis=-1, keepdims=True)
    u = (x * lax.rsqrt(ms + RMS_EPS) * g_ref[...]).astype(jnp.bfloat16)
    group = lambda m: matmul(u, w_ref[:, m * d_grp:(m + 1) * d_grp])

    glu_ref[...] = group(0) * _sigmoid(group(1))
    cg = group(2)
    cgate_ref[...] = cg * _sigmoid(cg)
    k_ref[...] = group(4).astype(jnp.bfloat16)
    sg = group(6)
    sgate_ref[...] = sg * _sigmoid(sg)

    qt = (group(3) * (HEAD_DIM ** -0.5)).T.astype(jnp.bfloat16)
    vt = group(5).T.astype(jnp.bfloat16)
    for c in range(qt_ref.shape[0]):
        qt_ref[c] = qt[:, c * ATT_BLOCK:(c + 1) * ATT_BLOCK]
        vt_ref[c] = vt[:, c * ATT_BLOCK:(c + 1) * ATT_BLOCK]


def _seq_rows(kind, tile, rows, first_row, n_rows, n_tiles):
    last_rows = first_row + n_rows - (n_tiles - 1) * rows
    if kind == "first":
        return first_row, 0, rows - first_row
    if kind == "last":
        return 0, n_rows - last_rows, last_rows
    return 0, pl.multiple_of(tile * rows - first_row, SUBLANES), rows


def _by_tile_kind(i, n_tiles, fn):
    pl.when(i == 0)(lambda: fn("first"))
    pl.when(jnp.logical_and(i > 0, i < n_tiles - 1))(lambda: fn("mid"))
    pl.when(i == n_tiles - 1)(lambda: fn("last"))


def _in_proj_first_kernel(x_hbm, meta_ref, g_ref, w_ref, h_ref, *rest, n_rows):
    out_refs, (xbuf_ref, sem) = rest[:-2], rest[-2:]
    b, i = pl.program_id(0), pl.program_id(1)
    n_b, n_t = pl.num_programs(0), pl.num_programs(1)
    first_row, rows = meta_ref.shape[0], h_ref.shape[0]
    step = b * n_t + i
    slot = step % 2

    def fetch(kind, batch, tile, to_slot):
        r0, s0, n = _seq_rows(kind, tile, rows, first_row, n_rows, n_t)
        return pltpu.make_async_copy(x_hbm.at[batch, pl.ds(s0, n)],
                                     xbuf_ref.at[to_slot, pl.ds(r0, n)], sem.at[to_slot])

    @pl.when(step == 0)
    def _():
        fetch("first", 0, 0, 0).start()

    @pl.when(step + 1 < n_b * n_t)
    def _():
        nxt = jnp.where(i + 1 < n_t, i + 1, 0)
        nxt_b = jnp.where(i + 1 < n_t, b, b + 1)
        _by_tile_kind(nxt, n_t, lambda kind: fetch(kind, nxt_b, nxt, 1 - slot).start())

    _by_tile_kind(i, n_t, lambda kind: fetch(kind, b, i, slot).wait())

    @pl.when(i == 0)
    def _():
        xbuf_ref[slot, 0:first_row, :] = meta_ref[...]

    valid = _seq_rows("last", i, rows, first_row, n_rows, n_t)[2]
    if valid < rows:
        @pl.when(i == n_t - 1)
        def _():
            xbuf_ref[slot, valid:, :] = jnp.zeros((rows - valid, xbuf_ref.shape[-1]),
                                                  xbuf_ref.dtype)

    x = xbuf_ref[slot]
    h_ref[...] = x
    _project(x, g_ref, w_ref, *out_refs)


def _projection_outputs(bsz, lp, d_grp):
    n_sub = ROW_TILE // ATT_BLOCK
    row_f32 = jax.ShapeDtypeStruct((bsz, lp, d_grp), jnp.float32)
    row_bf16 = jax.ShapeDtypeStruct((bsz, lp, d_grp), jnp.bfloat16)
    fm_bf16 = jax.ShapeDtypeStruct((bsz, lp // ATT_BLOCK, d_grp, ATT_BLOCK), jnp.bfloat16)
    row_spec = pl.BlockSpec((None, ROW_TILE, d_grp), lambda b, i: (b, i, 0))
    fm_spec = pl.BlockSpec((None, n_sub, d_grp, ATT_BLOCK), lambda b, i: (b, i, 0, 0))
    return ([row_f32, row_f32, row_bf16, row_f32, fm_bf16, fm_bf16],
            [row_spec, row_spec, row_spec, row_spec, fm_spec, fm_spec])


def _projection_inputs(pre_g, w_in, layer):
    d = pre_g.shape[-1]
    return ([pre_g, w_in],
            [pl.BlockSpec((None, 1, d), lambda b, i: (layer, 0, 0)),
             pl.BlockSpec((None,) + w_in.shape[1:], lambda b, i: (layer, 0, 0))])


def _in_proj_first(x, meta, lp, pre_g, w_in, d_grp):
    bsz, seq, d = x.shape
    n_tiles = lp // ROW_TILE
    first_row = meta.shape[0]
    assert n_tiles >= 3 and first_row % SUBLANES == 0 and 0 < first_row < ROW_TILE
    assert 0 < first_row + seq - (n_tiles - 1) * ROW_TILE <= ROW_TILE
    out_shape, out_specs = _projection_outputs(bsz, lp, d_grp)
    operands, specs = _projection_inputs(pre_g, w_in, 0)
    h_spec = pl.BlockSpec((None, ROW_TILE, d), lambda b, i: (b, i, 0))
    return pl.pallas_call(
        functools.partial(_in_proj_first_kernel, n_rows=seq),
        grid=(bsz, n_tiles),
        in_specs=[pl.BlockSpec(memory_space=pl.ANY),
                  pl.BlockSpec(meta.shape, lambda b, i: (0, 0))] + specs,
        out_specs=[h_spec] + out_specs,
        out_shape=[jax.ShapeDtypeStruct((bsz, lp, d), x.dtype)] + out_shape,
        scratch_shapes=[pltpu.VMEM((2, ROW_TILE, d), x.dtype), pltpu.SemaphoreType.DMA((2,))],
        compiler_params=pltpu.CompilerParams(
            dimension_semantics=("arbitrary", "arbitrary"),
            vmem_limit_bytes=VMEM_LIMIT),
        name="in_proj",
    )(x, meta, *operands)


def _conv_window(cur_ref, halo_ref, has_left, xs_ref):
    halo = halo_ref[...]
    xs_ref[0, 0:CONV_HALO, :] = jnp.where(has_left, halo, jnp.zeros_like(halo))
    xs_ref[0, CONV_HALO:, :] = cur_ref[...]
    span = ROW_TILE + CONV_HALO - SUBLANES
    for r in range(1, SUBLANES):
        xs_ref[r, 0:span, :] = xs_ref[0, r:r + span, :]


def _conv_chunk(ci, xs_ref, cw_ref, cb_ref, lg_ref, lb_ref, y_ref, after=None):
    r0 = ci * CONV_CHUNK
    first = CONV_HALO - (CONV_WIDTH - 1)
    cols = []
    for lc in range(cb_ref.shape[-1] // LANES):
        lanes = slice(lc * LANES, (lc + 1) * LANES)
        acc = jnp.broadcast_to(cb_ref[:, lanes], (CONV_CHUNK, LANES))
        if after is not None and lc == 0:
            acc = jnp.where(after[0], after[1][0:CONV_CHUNK, 0:LANES], acc)
        for j in range(CONV_WIDTH):
            tiles, r = divmod(first + j, SUBLANES)
            rows = slice(r0 + tiles * SUBLANES, r0 + tiles * SUBLANES + CONV_CHUNK)
            acc = acc + cw_ref[j:j + 1, lanes] * xs_ref[r, rows, lanes]
        cols.append(acc)
    acc = jnp.concatenate(cols, axis=-1)
    mu = jnp.mean(acc, axis=-1, keepdims=True)
    cen = acc - mu
    var = jnp.mean(cen * cen, axis=-1, keepdims=True)
    y = cen * lax.rsqrt(var + LN_EPS) * lg_ref[...] + lb_ref[...]
    y = (y * _sigmoid(y)).astype(jnp.bfloat16)
    y_ref[r0:r0 + CONV_CHUNK, :] = y
    return y


def _conv_gate(y_ref, wp_ref, bp_ref, cgate_ref):
    p = _dot(y_ref[...], wp_ref[...]) + bp_ref[...]
    return (p * cgate_ref[...]).astype(jnp.bfloat16)


def _conv_branch(cur_ref, halo_ref, cgate_ref, cw_ref, cb_ref, lg_ref, lb_ref, wp_ref, bp_ref,
                 xs_ref, y_ref):
    _conv_window(cur_ref, halo_ref, pl.program_id(1) > 0, xs_ref)
    for ci in range(ROW_TILE // CONV_CHUNK):
        _conv_chunk(ci, xs_ref, cw_ref, cb_ref, lg_ref, lb_ref, y_ref)
    return _conv_gate(y_ref, wp_ref, bp_ref, cgate_ref)


def _store_final_rows(h_new, out_hbm, buf_ref, sem, first_row, n_rows):
    b, i = pl.program_id(0), pl.program_id(1)
    n_b, n_t = pl.num_programs(0), pl.num_programs(1)
    rows = buf_ref.shape[0]

    def copy(kind, batch, tile):
        r0, s0, n = _seq_rows(kind, tile, rows, first_row, n_rows, n_t)
        return pltpu.make_async_copy(buf_ref.at[pl.ds(r0, n)], out_hbm.at[batch, pl.ds(s0, n)],
                                     sem.at[0])

    @pl.when(jnp.logical_and(i == 0, b > 0))
    def _():
        copy("last", b - 1, n_t - 1).wait()

    @pl.when(i == 1)
    def _():
        copy("first", b, 0).wait()

    @pl.when(i >= 2)
    def _():
        copy("mid", b, i - 1).wait()

    buf_ref[...] = h_new
    _by_tile_kind(i, n_t, lambda kind: copy(kind, b, i).start())

    @pl.when(jnp.logical_and(i == n_t - 1, b == n_b - 1))
    def _():
        copy("last", b, i).wait()


def _mix_out_kernel(*refs, has_next, final_rows):
    (cur_ref, halo_ref, cgate_ref, cw_ref, cb_ref, lg_ref, lb_ref, wp_ref, bp_ref,
     s_ref, h_ref, wo_ref, pg_ref) = refs[:13]
    rest = refs[13:]
    next_in, rest = (rest[:2], rest[2:]) if has_next else ((), rest)
    o_ref = rest[0]
    next_out = rest[1:7] if has_next else ()
    xs_ref, y_ref = rest[-2:] if final_rows is None else rest[-4:-2]

    c = _conv_branch(cur_ref, halo_ref, cgate_ref, cw_ref, cb_ref, lg_ref, lb_ref, wp_ref, bp_ref,
                     xs_ref, y_ref)
    d_conv = c.shape[-1]
    mixed = _dot(c, wo_ref[0:d_conv, :]) + _dot(s_ref[...], wo_ref[d_conv:, :])
    ms = jnp.mean(mixed * mixed, axis=-1, keepdims=True)
    h_new = h_ref[...] + mixed * lax.rsqrt(ms + RMS_EPS) * pg_ref[...]
    if final_rows is None:
        o_ref[...] = h_new
    else:
        _store_final_rows(h_new, o_ref, *rest[-2:], *final_rows)
    if has_next:
        _project(h_new, *next_in, *next_out)


def _mix_out(glu, cgate, s, h, conv_w, conv_b, ln_g, ln_b, w_pw2, b_pw2, w_out, post_g, layer,
             next_proj=None, final_rows=None):
    assert (next_proj is None) != (final_rows is None)
    bsz, lp, dc = glu.shape
    d = h.shape[-1]
    halo_per_tile = ROW_TILE // CONV_HALO
    vec_spec = pl.BlockSpec((None, 1, dc), lambda b, i: (layer, 0, 0))
    row_spec = pl.BlockSpec((None, ROW_TILE, dc), lambda b, i: (b, i, 0))
    h_spec = pl.BlockSpec((None, ROW_TILE, d), lambda b, i: (b, i, 0))
    operands = [glu, glu, cgate, conv_w, conv_b, ln_g, ln_b, w_pw2, b_pw2, s, h, w_out, post_g]
    in_specs = [
        row_spec,
        pl.BlockSpec((None, CONV_HALO, dc),
                     lambda b, i: (b, jnp.maximum(i * halo_per_tile - 1, 0), 0)),
        row_spec,
        pl.BlockSpec((None, CONV_WIDTH, dc), lambda b, i: (layer, 0, 0)),
        vec_spec, vec_spec, vec_spec,
        pl.BlockSpec((None, dc, dc), lambda b, i: (layer, 0, 0)),
        vec_spec,
        pl.BlockSpec((None, ROW_TILE, s.shape[-1]), lambda b, i: (b, i, 0)),
        h_spec,
        pl.BlockSpec((None,) + w_out.shape[1:], lambda b, i: (layer, 0, 0)),
        pl.BlockSpec((None, 1, d), lambda b, i: (layer, 0, 0)),
    ]
    scratch_shapes = [pltpu.VMEM((SUBLANES, ROW_TILE + CONV_HALO, dc), jnp.float32),
                      pltpu.VMEM((ROW_TILE, dc), jnp.bfloat16)]
    if next_proj is not None:
        out_shape = [jax.ShapeDtypeStruct((bsz, lp, d), jnp.float32)]
        out_specs = [h_spec]
        more_operands, more_specs = _projection_inputs(*next_proj, layer + 1)
        operands += more_operands
        in_specs += more_specs
        proj_shape, proj_specs = _projection_outputs(bsz, lp, dc)
        out_shape += proj_shape
        out_specs += proj_specs
        semantics = ("parallel", "parallel")
    else:
        first_row, n_rows = final_rows
        n_tiles = lp // ROW_TILE
        assert n_tiles >= 3 and first_row % SUBLANES == 0 and 0 < first_row < ROW_TILE
        assert 0 < first_row + n_rows - (n_tiles - 1) * ROW_TILE <= ROW_TILE
        out_shape = [jax.ShapeDtypeStruct((bsz, n_rows, d), jnp.float32)]
        out_specs = [pl.BlockSpec(memory_space=pl.ANY)]
        scratch_shapes += [pltpu.VMEM((ROW_TILE, d), jnp.float32), pltpu.SemaphoreType.DMA((1,))]
        semantics = ("arbitrary", "arbitrary")
    return pl.pallas_call(
        functools.partial(_mix_out_kernel, has_next=next_proj is not None, final_rows=final_rows),
        grid=(bsz, lp // ROW_TILE),
        in_specs=in_specs,
        out_specs=out_specs,
        out_shape=out_shape,
        scratch_shapes=scratch_shapes,
        compiler_params=pltpu.CompilerParams(
            dimension_semantics=semantics, vmem_limit_bytes=VMEM_LIMIT_MIX_OUT),
        name="mix_out",
    )(*operands)


def _log_gates(z):
    log_go = jnp.minimum(z, 0.0) - jnp.log(1.0 + jnp.exp(jnp.minimum(z, -z)))
    return log_go, log_go - z


def _split_bf16(x):
    hi = x.astype(jnp.bfloat16)
    lo = (x - hi.astype(jnp.float32)).astype(jnp.bfloat16)
    return hi, lo


def _add_rows8(x, c8):
    r, n = x.shape
    return (x.reshape(r // 8, 8, n) + c8[None]).reshape(r, n)


def _block_start(j):
    start = j * ATT_BLOCK
    return start if isinstance(start, int) else pl.multiple_of(start, ATT_BLOCK)


def _sb_attn_kernel(qt_ref, k_ref, vt_ref, sgate_ref, tri_ref, s_ref, acc_ref, carry_ref, *, n_blk):
    blk = ATT_BLOCK
    full = FIRST_STEP_BLOCKS
    cut = blk - OLDEST_ROWS
    tri = tri_ref[...]
    row = lax.broadcasted_iota(jnp.int32, (LANES, blk), 0)
    head_rows = (row < HEAD_DIM, row >= HEAD_DIM)
    key_row = lax.broadcasted_iota(jnp.int32, (blk, blk), 0)
    diag_visible = key_row < lax.broadcasted_iota(jnp.int32, (blk, blk), 1)
    head = lambda hh: slice(hh * HEAD_DIM, (hh + 1) * HEAD_DIM)
    lane_blk = lambda m: slice(m * blk, (m + 1) * blk)
    no_rows = jnp.zeros((cut, blk), jnp.bfloat16)

    def load_keys(j):
        return k_ref[pl.ds(_block_start(j), blk), :], vt_ref[j]

    def head_queries(i):
        qt = qt_ref[i]
        zero = jnp.zeros_like(qt)
        return [jnp.where(m, qt, zero) for m in head_rows]

    def write_out(u, i):
        rows = pl.ds(_block_start(i), blk)
        out = acc_ref[u].T
        s_ref[rows, :] = (out * sgate_ref[rows, :]).astype(jnp.bfloat16)

    def older_rows(u, i, j, n_rows):
        q_heads = head_queries(i)
        kb, vb = load_keys(j)
        wanted = key_row < n_rows
        for hh in range(2):
            z = _dot(kb, q_heads[hh])
            log_go, log_stay = _log_gates(z)
            log_stay = jnp.where(wanted, log_stay, 0.0)
            sums = _dot(tri, jnp.concatenate(_split_bf16(log_stay), axis=0))
            carry = carry_ref[u, hh]
            a = jnp.where(wanted, jnp.exp(_add_rows8(log_go + sums[0:blk], carry)), 0.0)
            acc_ref[u, head(hh), :] += _dot(vb[head(hh), :], a.astype(jnp.bfloat16))
            carry_ref[u, hh] = carry + sums[blk:blk + 8]

    def finish(base, n_q, live, stage):
        for u in range(n_q):
            i = base + u
            oldest = i - (full - 1)
            if isinstance(oldest, int) and oldest < 0:
                continue
            if stage == 0:
                @pl.when(jnp.logical_and(live[u] > 0, oldest >= 0))
                def _():
                    def cond(c):
                        j, go = c
                        return jnp.logical_and(j >= 0, go > 0)

                    def body(c):
                        j, _ = c
                        older_rows(u, i, j, jnp.where(j == oldest, cut, blk))
                        go = (jnp.max(carry_ref[u]) >= LOG_F32_UNDERFLOW).astype(jnp.int32)
                        return j - 1, go

                    lax.while_loop(cond, body, (jnp.asarray(oldest, jnp.int32), jnp.int32(1)))
        if stage == 1:
            for u in range(n_q):
                write_out(u, base + u)

    def q_blocks(base, depths, pending=None):
        n_q = len(depths)
        if pending is not None:
            finish(pending[0], n_q, pending[1], stage=0)
            finish(pending[0], n_q, pending[1], stage=1)
        q_heads = [head_queries(base + u) for u in range(n_q)]
        chains = [(u, n, hh) for u in range(n_q) for n in range(depths[u]) for hh in range(2)]
        partial = lambda c: c[1] == full - 1
        offsets = sorted({u - n for u, n, _ in chains})
        keys = {d: load_keys(base + d) for d in offsets}

        z = {}
        for d in offsets:
            users = [c for c in chains if c[0] - c[1] == d]
            wide = _dot(keys[d][0], jnp.concatenate([q_heads[u][hh] for u, _, hh in users], axis=1))
            for m, c in enumerate(users):
                z[c] = wide[cut:, lane_blk(m)] if partial(c) else wide[:, lane_blk(m)]

        log_go, log_stay, halves = {}, {}, []
        for c in chains:
            log_go[c], stay = _log_gates(z[c])
            if c[1] == 0:
                stay = jnp.where(diag_visible, stay, 0.0)
            log_stay[c] = stay
            hi, lo = _split_bf16(stay)
            halves.append(jnp.concatenate([no_rows, hi, no_rows, lo] if partial(c) else [hi, lo],
                                          axis=0))

        wide = _dot(tri, jnp.concatenate(halves, axis=1))
        sums = {c: wide[:, lane_blk(m)] for m, c in enumerate(chains)}

        last = {}
        for u in range(n_q):
            for hh in range(2):
                carry = None
                weights = []
                for n in range(depths[u]):
                    c = (u, n, hh)
                    arg = log_go[c] + (sums[c][cut:blk] if partial(c) else sums[c][0:blk])
                    if carry is not None:
                        arg = _add_rows8(arg, carry)
                    e = jnp.exp(arg)
                    if n == 0:
                        e = jnp.where(diag_visible, e, 0.0)
                    if partial(c):
                        weights.append(no_rows)
                    weights.append(e.astype(jnp.bfloat16))
                    total = sums[c][blk:blk + 8]
                    carry = total if carry is None else carry + total
                values = [keys[u - n][1][head(hh), :] for n in range(depths[u])]
                acc_ref[u, head(hh), :] = _dot(jnp.concatenate(values, axis=1),
                                               jnp.concatenate(weights, axis=0))
                carry_ref[u, hh] = carry
                last[u, hh] = carry

        return tuple(
            (jnp.max(jnp.maximum(last[u, 0], last[u, 1])) >= LOG_F32_UNDERFLOW).astype(jnp.int32)
            for u in range(n_q))

    unroll = ATT_UNROLL
    lead = full - 1
    lead += (n_blk - lead) % unroll
    if lead < unroll:
        lead += unroll
    assert lead <= n_blk and (n_blk - lead) % unroll == 0
    starts = ([0] if lead % unroll else []) + list(range(lead % unroll, lead, unroll))
    for g, base in enumerate(starts):
        end = starts[g + 1] if g + 1 < len(starts) else lead
        live = q_blocks(base, [min(i + 1, full) for i in range(base, end)])
        if end < lead:
            finish(base, end - base, live, stage=0)
            finish(base, end - base, live, stage=1)

    def loop_body(t, live):
        base = lead + t * unroll
        return q_blocks(base, [full] * unroll, pending=(base - unroll, live))

    live = lax.fori_loop(0, (n_blk - lead) // unroll, loop_body, live)
    finish(n_blk - unroll, unroll, live, stage=0)
    finish(n_blk - unroll, unroll, live, stage=1)


def _sb_attn(qt, k, vt, sgate, tri):
    bsz, n_blk, d_sb, blk = qt.shape
    lp = k.shape[1]
    n_pairs = d_sb // LANES
    return pl.pallas_call(
        functools.partial(_sb_attn_kernel, n_blk=n_blk),
        grid=(bsz, n_pairs),
        in_specs=[
            pl.BlockSpec((None, n_blk, LANES, blk), lambda b, p: (b, 0, p, 0)),
            pl.BlockSpec((None, lp, LANES), lambda b, p: (b, 0, p)),
            pl.BlockSpec((None, n_blk, LANES, blk), lambda b, p: (b, 0, p, 0)),
            pl.BlockSpec((None, lp, LANES), lambda b, p: (b, 0, p)),
            pl.BlockSpec(tri.shape, lambda b, p: (0, 0)),
        ],
        out_specs=pl.BlockSpec((None, lp, LANES), lambda b, p: (b, 0, p)),
        out_shape=jax.ShapeDtypeStruct((bsz, lp, d_sb), jnp.bfloat16),
        scratch_shapes=[pltpu.VMEM((ATT_UNROLL, LANES, blk), jnp.float32),
                        pltpu.VMEM((ATT_UNROLL, 2, 8, blk), jnp.float32)],
        compiler_params=pltpu.CompilerParams(
            dimension_semantics=("parallel", "parallel"), vmem_limit_bytes=VMEM_LIMIT),
        name="sb_attn",
    )(qt, k, vt, sgate, tri)


def _mix_next_kernel(on_ref, cur_ref, halo_ref, cgate_ref, cw_ref, cb_ref, lg_ref, lb_ref, wp_ref,
                     bp_ref, s_ref, h_ref, wo_ref, pg_ref, g_ref, w_ref,
                     o_ref, glu_ref, cgo_ref, k_ref, sgate_ref, qt_ref, vt_ref,
                     xs_ref, y_ref, c_ref, *, n_tiles):
    t = pl.program_id(0)
    conv_tile = jnp.minimum(t, pl.num_programs(0) - 2)
    never = on_ref[0] == 0
    n_chunks = ROW_TILE // CONV_CHUNK
    d_conv = c_ref.shape[-1]

    @pl.when(t == 0)
    def _():
        c_ref[...] = jnp.zeros_like(c_ref)

    c = c_ref[...]
    _conv_window(cur_ref, halo_ref, conv_tile % n_tiles > 0, xs_ref)
    chunks = []

    def matmul(lhs, rhs):
        p = len(matmul.done)
        if TIE_BACK <= p < TIE_BACK + len(chunks):
            tie = jnp.where(never, chunks[p - TIE_BACK][0:16, 0:LANES], lhs[0:16, 0:LANES])
            lhs = jnp.concatenate(
                [jnp.concatenate([tie, lhs[0:16, LANES:]], axis=1), lhs[16:]], axis=0)
        res = _dot(lhs, rhs)
        matmul.done.append(p)
        if p < n_chunks:
            chunks.append(_conv_chunk(p, xs_ref, cw_ref, cb_ref, lg_ref, lb_ref, y_ref,
                                      after=(never, res)))
        return res

    matmul.done = []
    mixed = matmul(c, wo_ref[0:d_conv, :]) + matmul(s_ref[...], wo_ref[d_conv:, :])
    ms = jnp.mean(mixed * mixed, axis=-1, keepdims=True)
    h_new = h_ref[...] + mixed * lax.rsqrt(ms + RMS_EPS) * pg_ref[...]
    o_ref[...] = h_new
    _project(h_new, g_ref, w_ref, glu_ref, cgo_ref, k_ref, sgate_ref, qt_ref, vt_ref, matmul=matmul)
    assert len(chunks) == n_chunks
    c_ref[...] = _conv_gate(y_ref, wp_ref, bp_ref, cgate_ref)


def _mix_next(glu, cgate, s, h, conv_w, conv_b, ln_g, ln_b, w_pw2, b_pw2, w_out, post_g,
              pre_g, w_in, layer):
    bsz, lp, dc = glu.shape
    d = h.shape[-1]
    n_tiles = lp // ROW_TILE
    n_sub = ROW_TILE // ATT_BLOCK
    halo_per_tile = ROW_TILE // CONV_HALO
    last = bsz * n_tiles - 1

    def conv_tile(t):
        t = jnp.minimum(t, last)
        return t // n_tiles, t % n_tiles

    def proj_tile(t):
        t = jnp.maximum(t - 1, 0)
        return t // n_tiles, t % n_tiles

    def halo_block(t):
        b, i = conv_tile(t)
        return b, jnp.maximum(i * halo_per_tile - 1, 0), 0

    conv_rows = pl.BlockSpec((None, ROW_TILE, dc), lambda t: conv_tile(t) + (0,))
    proj_rows = lambda width: pl.BlockSpec((None, ROW_TILE, width), lambda t: proj_tile(t) + (0,))
    proj_fm = pl.BlockSpec((None, n_sub, dc, ATT_BLOCK), lambda t: proj_tile(t) + (0, 0))
    vec = lambda width, lyr: pl.BlockSpec((None, 1, width), lambda t: (lyr, 0, 0))
    whole = lambda a, lyr: pl.BlockSpec((None,) + a.shape[1:], lambda t: (lyr, 0, 0))
    row_f32 = jax.ShapeDtypeStruct((bsz, lp, dc), jnp.float32)
    row_bf16 = jax.ShapeDtypeStruct((bsz, lp, dc), jnp.bfloat16)
    fm_bf16 = jax.ShapeDtypeStruct((bsz, lp // ATT_BLOCK, dc, ATT_BLOCK), jnp.bfloat16)
    return pl.pallas_call(
        functools.partial(_mix_next_kernel, n_tiles=n_tiles),
        grid=(bsz * n_tiles + 1,),
        in_specs=[
            pl.BlockSpec(memory_space=pltpu.SMEM),
            conv_rows,
            pl.BlockSpec((None, CONV_HALO, dc), halo_block),
            conv_rows,
            whole(conv_w, layer), vec(dc, layer), vec(dc, layer), vec(dc, layer),
            whole(w_pw2, layer), vec(dc, layer),
            proj_rows(s.shape[-1]), proj_rows(d),
            whole(w_out, layer), vec(d, layer),
            vec(d, layer + 1), whole(w_in, layer + 1),
        ],
        out_specs=[proj_rows(d), proj_rows(dc), proj_rows(dc), proj_rows(dc), proj_rows(dc),
                   proj_fm, proj_fm],
        out_shape=[jax.ShapeDtypeStruct((bsz, lp, d), jnp.float32),
                   row_f32, row_f32, row_bf16, row_f32, fm_bf16, fm_bf16],
        scratch_shapes=[pltpu.VMEM((SUBLANES, ROW_TILE + CONV_HALO, dc), jnp.float32),
                        pltpu.VMEM((ROW_TILE, dc), jnp.bfloat16),
                        pltpu.VMEM((ROW_TILE, dc), jnp.bfloat16)],
        compiler_params=pltpu.CompilerParams(
            dimension_semantics=("arbitrary",),
            vmem_limit_bytes=VMEM_LIMIT_MIX_OUT),
        name="mix_next",
    )(jnp.ones((1,), jnp.int32), glu, glu, cgate, conv_w, conv_b, ln_g, ln_b, w_pw2, b_pw2,
      s, h, w_out, post_g, pre_g, w_in)


def _suffix_sum_matrix():
    r = lax.broadcasted_iota(jnp.int32, (TRI_ROWS, 2 * ATT_BLOCK), 0)
    c = lax.broadcasted_iota(jnp.int32, (TRI_ROWS, 2 * ATT_BLOCK), 1) % ATT_BLOCK
    return jnp.where((r >= ATT_BLOCK) | (c > r), 1.0, 0.0).astype(jnp.bfloat16)


def kernel(x, meta_tokens, pre_norm_g, post_norm_g, w_in, conv_w, conv_b, conv_ln_g, conv_ln_b,
           w_pw2, b_pw2, w_out):
    bsz, seq, d = x.shape
    depth = w_in.shape[0]
    d_conv = conv_w.shape[-1]
    d_sb = N_HEADS * HEAD_DIM
    assert w_in.shape[-1] == 3 * d_conv + 4 * d_sb and d_conv == d_sb
    assert ROW_TILE % ATT_BLOCK == 0 and ROW_TILE % CONV_CHUNK == 0
    assert CONV_HALO % SUBLANES == 0 and CONV_HALO >= CONV_WIDTH - 1 and ROW_TILE % CONV_HALO == 0

    length = N_META + seq
    lp = -(-length // ROW_TILE) * ROW_TILE

    bf16 = jnp.bfloat16
    w_in_b = w_in.astype(bf16)
    w_pw2_b = w_pw2.astype(bf16)
    w_out_b = w_out.astype(bf16)
    vec = lambda a: a.reshape(depth, 1, a.shape[-1])
    pre_g, post_g = vec(pre_norm_g), vec(post_norm_g)
    conv_b3, ln_g3, ln_b3, b_pw3 = vec(conv_b), vec(conv_ln_g), vec(conv_ln_b), vec(b_pw2)
    tri = _suffix_sum_matrix()

    h, glu, cgate, k, sgate, qt, vt = _in_proj_first(
        x, meta_tokens.astype(x.dtype), lp, pre_g, w_in_b, d_conv)
    conv_params = (conv_w, conv_b3, ln_g3, ln_b3, w_pw2_b, b_pw3)
    for layer in range(depth - 1):
        s = _sb_attn(qt, k, vt, sgate, tri)
        h, glu, cgate, k, sgate, qt, vt = _mix_next(
            glu, cgate, s, h, *conv_params, w_out_b, post_g, pre_g, w_in_b, layer)
    s = _sb_attn(qt, k, vt, sgate, tri)
    return _mix_out(glu, cgate, s, h, *conv_params, w_out_b, post_g, depth - 1,
                    final_rows=(N_META, seq))[0]
```

```python
import functools

import jax
import jax.numpy as jnp
from jax import lax
from jax.experimental import pallas as pl
from jax.experimental.pallas import tpu as pltpu

N_META = 16
N_HEADS = 8
HEAD_DIM = 64
CONV_WIDTH = 31
RMS_EPS = 1e-6
LN_EPS = 1e-5
LOG2_E = 1.4426950408889634

LANES = 128
SUBLANES = 8
ROW_TILE = 384
ATT_BLOCK = 128
FIRST_STEP_BLOCKS = 3
OLDEST_ROWS = 32
ATT_UNROLL = 6
CONV_HALO = 32
CONV_CHUNK = 32
TIE_BACK = 2
CONV_PIECES = 6
TRI_ROWS = ATT_BLOCK + 16
LOG_F32_UNDERFLOW = -104.0
VMEM_LIMIT = 48 * 1024 * 1024
VMEM_LIMIT_MIX_OUT = 58 * 1024 * 1024


def _sigmoid(x):
    return 1.0 / (1.0 + jnp.exp2(x * -LOG2_E))


def _dot(a, b):
    return jnp.dot(a, b, preferred_element_type=jnp.float32)


def _project(x, g_ref, w_ref, glu_ref, cgate_ref, k_ref, sgate_ref, qt_ref, vt_ref, matmul=None):
    matmul = matmul or _dot
    d_grp = glu_ref.shape[-1]
    ms = jnp.mean(x * x, axis=-1, keepdims=True)
    u = (x * lax.rsqrt(ms + RMS_EPS) * g_ref[...]).astype(jnp.bfloat16)
    group = lambda m: matmul(u, w_ref[:, m * d_grp:(m + 1) * d_grp])

    glu_ref[...] = group(0) * _sigmoid(group(1))
    cg = group(2)
    cgate_ref[...] = cg * _sigmoid(cg)
    k_ref[...] = group(4).astype(jnp.bfloat16)
    sg = group(6)
    sgate_ref[...] = sg * _sigmoid(sg)

    qt = (group(3) * (HEAD_DIM ** -0.5)).T.astype(jnp.bfloat16)
    vt = group(5).T.astype(jnp.bfloat16)
    for c in range(qt_ref.shape[0]):
        qt_ref[c] = qt[:, c * ATT_BLOCK:(c + 1) * ATT_BLOCK]
        vt_ref[c] = vt[:, c * ATT_BLOCK:(c + 1) * ATT_BLOCK]


def _seq_rows(kind, tile, rows, first_row, n_rows, n_tiles):
    last_rows = first_row + n_rows - (n_tiles - 1) * rows
    if kind == "first":
        return first_row, 0, rows - first_row
    if kind == "last":
        return 0, n_rows - last_rows, last_rows
    return 0, pl.multiple_of(tile * rows - first_row, SUBLANES), rows


def _by_tile_kind(i, n_tiles, fn):
    pl.when(i == 0)(lambda: fn("first"))
    pl.when(jnp.logical_and(i > 0, i < n_tiles - 1))(lambda: fn("mid"))
    pl.when(i == n_tiles - 1)(lambda: fn("last"))


def _in_proj_first_kernel(x_hbm, meta_ref, g_ref, w_ref, h_ref, *rest, n_rows):
    out_refs, (xbuf_ref, sem) = rest[:-2], rest[-2:]
    b, i = pl.program_id(0), pl.program_id(1)
    n_b, n_t = pl.num_programs(0), pl.num_programs(1)
    first_row, rows = meta_ref.shape[0], h_ref.shape[0]
    step = b * n_t + i
    slot = step % 2

    def fetch(kind, batch, tile, to_slot):
        r0, s0, n = _seq_rows(kind, tile, rows, first_row, n_rows, n_t)
        return pltpu.make_async_copy(x_hbm.at[batch, pl.ds(s0, n)],
                                     xbuf_ref.at[to_slot, pl.ds(r0, n)], sem.at[to_slot])

    @pl.when(step == 0)
    def _():
        fetch("first", 0, 0, 0).start()

    @pl.when(step + 1 < n_b * n_t)
    def _():
        nxt = jnp.where(i + 1 < n_t, i + 1, 0)
        nxt_b = jnp.where(i + 1 < n_t, b, b + 1)
        _by_tile_kind(nxt, n_t, lambda kind: fetch(kind, nxt_b, nxt, 1 - slot).start())

    _by_tile_kind(i, n_t, lambda kind: fetch(kind, b, i, slot).wait())

    @pl.when(i == 0)
    def _():
        xbuf_ref[slot, 0:first_row, :] = meta_ref[...]

    valid = _seq_rows("last", i, rows, first_row, n_rows, n_t)[2]
    if valid < rows:
        @pl.when(i == n_t - 1)
        def _():
            xbuf_ref[slot, valid:, :] = jnp.zeros((rows - valid, xbuf_ref.shape[-1]),
                                                  xbuf_ref.dtype)

    x = xbuf_ref[slot]
    h_ref[...] = x
    _project(x, g_ref, w_ref, *out_refs)


def _projection_outputs(bsz, lp, d_grp):
    n_sub = ROW_TILE // ATT_BLOCK
    row_f32 = jax.ShapeDtypeStruct((bsz, lp, d_grp), jnp.float32)
    row_bf16 = jax.ShapeDtypeStruct((bsz, lp, d_grp), jnp.bfloat16)
    fm_bf16 = jax.ShapeDtypeStruct((bsz, lp // ATT_BLOCK, d_grp, ATT_BLOCK), jnp.bfloat16)
    row_spec = pl.BlockSpec((None, ROW_TILE, d_grp), lambda b, i: (b, i, 0))
    fm_spec = pl.BlockSpec((None, n_sub, d_grp, ATT_BLOCK), lambda b, i: (b, i, 0, 0))
    return ([row_f32, row_f32, row_bf16, row_f32, fm_bf16, fm_bf16],
            [row_spec, row_spec, row_spec, row_spec, fm_spec, fm_spec])


def _projection_inputs(pre_g, w_in, layer):
    d = pre_g.shape[-1]
    return ([pre_g, w_in],
            [pl.BlockSpec((None, 1, d), lambda b, i: (layer, 0, 0)),
             pl.BlockSpec((None,) + w_in.shape[1:], lambda b, i: (layer, 0, 0))])


def _in_proj_first(x, meta, lp, pre_g, w_in, d_grp):
    bsz, seq, d = x.shape
    n_tiles = lp // ROW_TILE
    first_row = meta.shape[0]
    assert n_tiles >= 3 and first_row % SUBLANES == 0 and 0 < first_row < ROW_TILE
    assert 0 < first_row + seq - (n_tiles - 1) * ROW_TILE <= ROW_TILE
    out_shape, out_specs = _projection_outputs(bsz, lp, d_grp)
    operands, specs = _projection_inputs(pre_g, w_in, 0)
    h_spec = pl.BlockSpec((None, ROW_TILE, d), lambda b, i: (b, i, 0))
    return pl.pallas_call(
        functools.partial(_in_proj_first_kernel, n_rows=seq),
        grid=(bsz, n_tiles),
        in_specs=[pl.BlockSpec(memory_space=pl.ANY),
                  pl.BlockSpec(meta.shape, lambda b, i: (0, 0))] + specs,
        out_specs=[h_spec] + out_specs,
        out_shape=[jax.ShapeDtypeStruct((bsz, lp, d), x.dtype)] + out_shape,
        scratch_shapes=[pltpu.VMEM((2, ROW_TILE, d), x.dtype), pltpu.SemaphoreType.DMA((2,))],
        compiler_params=pltpu.CompilerParams(
            dimension_semantics=("arbitrary", "arbitrary"),
            vmem_limit_bytes=VMEM_LIMIT),
        name="in_proj",
    )(x, meta, *operands)


def _conv_window(cur_ref, halo_ref, has_left, xs_ref):
    halo = halo_ref[...]
    xs_ref[0, 0:CONV_HALO, :] = jnp.where(has_left, halo, jnp.zeros_like(halo))
    xs_ref[0, CONV_HALO:, :] = cur_ref[...]
    span = ROW_TILE + CONV_HALO - SUBLANES
    for r in range(1, SUBLANES):
        xs_ref[r, 0:span, :] = xs_ref[0, r:r + span, :]


def _conv_chunk(ci, xs_ref, cw_ref, cb_ref, lg_ref, lb_ref, y_ref, after=None):
    r0 = ci * CONV_CHUNK
    first = CONV_HALO - (CONV_WIDTH - 1)
    cols = []
    for lc in range(cb_ref.shape[-1] // LANES):
        lanes = slice(lc * LANES, (lc + 1) * LANES)
        acc = jnp.broadcast_to(cb_ref[:, lanes], (CONV_CHUNK, LANES))
        if after is not None and lc == 0:
            acc = jnp.where(after[0], after[1][0:CONV_CHUNK, 0:LANES], acc)
        for j in range(CONV_WIDTH):
            tiles, r = divmod(first + j, SUBLANES)
            rows = slice(r0 + tiles * SUBLANES, r0 + tiles * SUBLANES + CONV_CHUNK)
            acc = acc + cw_ref[j:j + 1, lanes] * xs_ref[r, rows, lanes]
        cols.append(acc)
    acc = jnp.concatenate(cols, axis=-1)
    mu = jnp.mean(acc, axis=-1, keepdims=True)
    cen = acc - mu
    var = jnp.mean(cen * cen, axis=-1, keepdims=True)
    y = cen * lax.rsqrt(var + LN_EPS) * lg_ref[...] + lb_ref[...]
    y = (y * _sigmoid(y)).astype(jnp.bfloat16)
    y_ref[r0:r0 + CONV_CHUNK, :] = y
    return y


def _conv_gate(y_ref, wp_ref, bp_ref, cgate_ref):
    p = _dot(y_ref[...], wp_ref[...]) + bp_ref[...]
    return (p * cgate_ref[...]).astype(jnp.bfloat16)


def _conv_branch(cur_ref, halo_ref, cgate_ref, cw_ref, cb_ref, lg_ref, lb_ref, wp_ref, bp_ref,
                 xs_ref, y_ref):
    _conv_window(cur_ref, halo_ref, pl.program_id(1) > 0, xs_ref)
    for ci in range(ROW_TILE // CONV_CHUNK):
        _conv_chunk(ci, xs_ref, cw_ref, cb_ref, lg_ref, lb_ref, y_ref)
    return _conv_gate(y_ref, wp_ref, bp_ref, cgate_ref)


def _store_final_rows(h_new, out_hbm, buf_ref, sem, first_row, n_rows):
    b, i = pl.program_id(0), pl.program_id(1)
    n_b, n_t = pl.num_programs(0), pl.num_programs(1)
    rows = buf_ref.shape[0]

    def copy(kind, batch, tile):
        r0, s0, n = _seq_rows(kind, tile, rows, first_row, n_rows, n_t)
        return pltpu.make_async_copy(buf_ref.at[pl.ds(r0, n)], out_hbm.at[batch, pl.ds(s0, n)],
                                     sem.at[0])

    @pl.when(jnp.logical_and(i == 0, b > 0))
    def _():
        copy("last", b - 1, n_t - 1).wait()

    @pl.when(i == 1)
    def _():
        copy("first", b, 0).wait()

    @pl.when(i >= 2)
    def _():
        copy("mid", b, i - 1).wait()

    buf_ref[...] = h_new
    _by_tile_kind(i, n_t, lambda kind: copy(kind, b, i).start())

    @pl.when(jnp.logical_and(i == n_t - 1, b == n_b - 1))
    def _():
        copy("last", b, i).wait()


def _mix_out_kernel(*refs, has_next, final_rows):
    (cur_ref, halo_ref, cgate_ref, cw_ref, cb_ref, lg_ref, lb_ref, wp_ref, bp_ref,
     s_ref, h_ref, wo_ref, pg_ref) = refs[:13]
    rest = refs[13:]
    next_in, rest = (rest[:2], rest[2:]) if has_next else ((), rest)
    o_ref = rest[0]
    next_out = rest[1:7] if has_next else ()
    xs_ref, y_ref = rest[-2:] if final_rows is None else rest[-4:-2]

    c = _conv_branch(cur_ref, halo_ref, cgate_ref, cw_ref, cb_ref, lg_ref, lb_ref, wp_ref, bp_ref,
                     xs_ref, y_ref)
    d_conv = c.shape[-1]
    mixed = _dot(c, wo_ref[0:d_conv, :]) + _dot(s_ref[...], wo_ref[d_conv:, :])
    ms = jnp.mean(mixed * mixed, axis=-1, keepdims=True)
    h_new = h_ref[...] + mixed * lax.rsqrt(ms + RMS_EPS) * pg_ref[...]
    if final_rows is None:
        o_ref[...] = h_new
    else:
        _store_final_rows(h_new, o_ref, *rest[-2:], *final_rows)
    if has_next:
        _project(h_new, *next_in, *next_out)


def _mix_out(glu, cgate, s, h, conv_w, conv_b, ln_g, ln_b, w_pw2, b_pw2, w_out, post_g, layer,
             next_proj=None, final_rows=None):
    assert (next_proj is None) != (final_rows is None)
    bsz, lp, dc = glu.shape
    d = h.shape[-1]
    halo_per_tile = ROW_TILE // CONV_HALO
    vec_spec = pl.BlockSpec((None, 1, dc), lambda b, i: (layer, 0, 0))
    row_spec = pl.BlockSpec((None, ROW_TILE, dc), lambda b, i: (b, i, 0))
    h_spec = pl.BlockSpec((None, ROW_TILE, d), lambda b, i: (b, i, 0))
    operands = [glu, glu, cgate, conv_w, conv_b, ln_g, ln_b, w_pw2, b_pw2, s, h, w_out, post_g]
    in_specs = [
        row_spec,
        pl.BlockSpec((None, CONV_HALO, dc),
                     lambda b, i: (b, jnp.maximum(i * halo_per_tile - 1, 0), 0)),
        row_spec,
        pl.BlockSpec((None, CONV_WIDTH, dc), lambda b, i: (layer, 0, 0)),
        vec_spec, vec_spec, vec_spec,
        pl.BlockSpec((None, dc, dc), lambda b, i: (layer, 0, 0)),
        vec_spec,
        pl.BlockSpec((None, ROW_TILE, s.shape[-1]), lambda b, i: (b, i, 0)),
        h_spec,
        pl.BlockSpec((None,) + w_out.shape[1:], lambda b, i: (layer, 0, 0)),
        pl.BlockSpec((None, 1, d), lambda b, i: (layer, 0, 0)),
    ]
    scratch_shapes = [pltpu.VMEM((SUBLANES, ROW_TILE + CONV_HALO, dc), jnp.float32),
                      pltpu.VMEM((ROW_TILE, dc), jnp.bfloat16)]
    if next_proj is not None:
        out_shape = [jax.ShapeDtypeStruct((bsz, lp, d), jnp.float32)]
        out_specs = [h_spec]
        more_operands, more_specs = _projection_inputs(*next_proj, layer + 1)
        operands += more_operands
        in_specs += more_specs
        proj_shape, proj_specs = _projection_outputs(bsz, lp, dc)
        out_shape += proj_shape
        out_specs += proj_specs
        semantics = ("parallel", "parallel")
    else:
        first_row, n_rows = final_rows
        n_tiles = lp // ROW_TILE
        assert n_tiles >= 3 and first_row % SUBLANES == 0 and 0 < first_row < ROW_TILE
        assert 0 < first_row + n_rows - (n_tiles - 1) * ROW_TILE <= ROW_TILE
        out_shape = [jax.ShapeDtypeStruct((bsz, n_rows, d), jnp.float32)]
        out_specs = [pl.BlockSpec(memory_space=pl.ANY)]
        scratch_shapes += [pltpu.VMEM((ROW_TILE, d), jnp.float32), pltpu.SemaphoreType.DMA((1,))]
        semantics = ("arbitrary", "arbitrary")
    return pl.pallas_call(
        functools.partial(_mix_out_kernel, has_next=next_proj is not None, final_rows=final_rows),
        grid=(bsz, lp // ROW_TILE),
        in_specs=in_specs,
        out_specs=out_specs,
        out_shape=out_shape,
        scratch_shapes=scratch_shapes,
        compiler_params=pltpu.CompilerParams(
            dimension_semantics=semantics, vmem_limit_bytes=VMEM_LIMIT_MIX_OUT),
        name="mix_out",
    )(*operands)


def _log_gates(z):
    log_go = jnp.minimum(z, 0.0) - jnp.log(1.0 + jnp.exp(jnp.minimum(z, -z)))
    return log_go, log_go - z


def _split_bf16(x):
    hi = x.astype(jnp.bfloat16)
    lo = (x - hi.astype(jnp.float32)).astype(jnp.bfloat16)
    return hi, lo


def _add_rows8(x, c8):
    r, n = x.shape
    return (x.reshape(r // 8, 8, n) + c8[None]).reshape(r, n)


def _block_start(j):
    start = j * ATT_BLOCK
    return start if isinstance(start, int) else pl.multiple_of(start, ATT_BLOCK)


def _sb_attn_kernel(qt_ref, k_ref, vt_ref, sgate_ref, tri_ref, s_ref, acc_ref, carry_ref, *, n_blk):
    blk = ATT_BLOCK
    full = FIRST_STEP_BLOCKS
    cut = blk - OLDEST_ROWS
    tri = tri_ref[...]
    row = lax.broadcasted_iota(jnp.int32, (LANES, blk), 0)
    head_rows = (row < HEAD_DIM, row >= HEAD_DIM)
    key_row = lax.broadcasted_iota(jnp.int32, (blk, blk), 0)
    diag_visible = key_row < lax.broadcasted_iota(jnp.int32, (blk, blk), 1)
    head = lambda hh: slice(hh * HEAD_DIM, (hh + 1) * HEAD_DIM)
    lane_blk = lambda m: slice(m * blk, (m + 1) * blk)
    no_rows = jnp.zeros((cut, blk), jnp.bfloat16)

    def load_keys(j):
        return k_ref[pl.ds(_block_start(j), blk), :], vt_ref[j]

    def head_queries(i):
        qt = qt_ref[i]
        zero = jnp.zeros_like(qt)
        return [jnp.where(m, qt, zero) for m in head_rows]

    def write_out(u, i):
        rows = pl.ds(_block_start(i), blk)
        out = acc_ref[u].T
        s_ref[rows, :] = (out * sgate_ref[rows, :]).astype(jnp.bfloat16)

    def older_rows(u, i, j, n_rows):
        q_heads = head_queries(i)
        kb, vb = load_keys(j)
        wanted = key_row < n_rows
        for hh in range(2):
            z = _dot(kb, q_heads[hh])
            log_go, log_stay = _log_gates(z)
            log_stay = jnp.where(wanted, log_stay, 0.0)
            sums = _dot(tri, jnp.concatenate(_split_bf16(log_stay), axis=0))
            carry = carry_ref[u, hh]
            a = jnp.where(wanted, jnp.exp(_add_rows8(log_go + sums[0:blk], carry)), 0.0)
            acc_ref[u, head(hh), :] += _dot(vb[head(hh), :], a.astype(jnp.bfloat16))
            carry_ref[u, hh] = carry + sums[blk:blk + 8]

    def finish(base, n_q, live, stage):
        for u in range(n_q):
            i = base + u
            oldest = i - (full - 1)
            if isinstance(oldest, int) and oldest < 0:
                continue
            if stage == 0:
                @pl.when(jnp.logical_and(live[u] > 0, oldest >= 0))
                def _():
                    def cond(c):
                        j, go = c
                        return jnp.logical_and(j >= 0, go > 0)

                    def body(c):
                        j, _ = c
                        older_rows(u, i, j, jnp.where(j == oldest, cut, blk))
                        go = (jnp.max(carry_ref[u]) >= LOG_F32_UNDERFLOW).astype(jnp.int32)
                        return j - 1, go

                    lax.while_loop(cond, body, (jnp.asarray(oldest, jnp.int32), jnp.int32(1)))
        if stage == 1:
            for u in range(n_q):
                write_out(u, base + u)

    def q_blocks(base, depths, pending=None):
        n_q = len(depths)
        if pending is not None:
            finish(pending[0], n_q, pending[1], stage=0)
            finish(pending[0], n_q, pending[1], stage=1)
        q_heads = [head_queries(base + u) for u in range(n_q)]
        chains = [(u, n, hh) for u in range(n_q) for n in range(depths[u]) for hh in range(2)]
        partial = lambda c: c[1] == full - 1
        offsets = sorted({u - n for u, n, _ in chains})
        keys = {d: load_keys(base + d) for d in offsets}

        z = {}
        for d in offsets:
            users = [c for c in chains if c[0] - c[1] == d]
            wide = _dot(keys[d][0], jnp.concatenate([q_heads[u][hh] for u, _, hh in users], axis=1))
            for m, c in enumerate(users):
                z[c] = wide[cut:, lane_blk(m)] if partial(c) else wide[:, lane_blk(m)]

        log_go, log_stay, halves = {}, {}, []
        for c in chains:
            log_go[c], stay = _log_gates(z[c])
            if c[1] == 0:
                stay = jnp.where(diag_visible, stay, 0.0)
            log_stay[c] = stay
            hi, lo = _split_bf16(stay)
            halves.append(jnp.concatenate([no_rows, hi, no_rows, lo] if partial(c) else [hi, lo],
                                          axis=0))

        wide = _dot(tri, jnp.concatenate(halves, axis=1))
        sums = {c: wide[:, lane_blk(m)] for m, c in enumerate(chains)}

        last = {}
        for u in range(n_q):
            for hh in range(2):
                carry = None
                weights = []
                for n in range(depths[u]):
                    c = (u, n, hh)
                    arg = log_go[c] + (sums[c][cut:blk] if partial(c) else sums[c][0:blk])
                    if carry is not None:
                        arg = _add_rows8(arg, carry)
                    e = jnp.exp(arg)
                    if n == 0:
                        e = jnp.where(diag_visible, e, 0.0)
                    if partial(c):
                        weights.append(no_rows)
                    weights.append(e.astype(jnp.bfloat16))
                    total = sums[c][blk:blk + 8]
                    carry = total if carry is None else carry + total
                values = [keys[u - n][1][head(hh), :] for n in range(depths[u])]
                acc_ref[u, head(hh), :] = _dot(jnp.concatenate(values, axis=1),
                                               jnp.concatenate(weights, axis=0))
                carry_ref[u, hh] = carry
                last[u, hh] = carry

        return tuple(
            (jnp.max(jnp.maximum(last[u, 0], last[u, 1])) >= LOG_F32_UNDERFLOW).astype(jnp.int32)
            for u in range(n_q))

    unroll = ATT_UNROLL
    lead = full - 1
    lead += (n_blk - lead) % unroll
    if lead < unroll:
        lead += unroll
    assert lead <= n_blk and (n_blk - lead) % unroll == 0
    starts = ([0] if lead % unroll else []) + list(range(lead % unroll, lead, unroll))
    for g, base in enumerate(starts):
        end = starts[g + 1] if g + 1 < len(starts) else lead
        live = q_blocks(base, [min(i + 1, full) for i in range(base, end)])
        if end < lead:
            finish(base, end - base, live, stage=0)
            finish(base, end - base, live, stage=1)

    def loop_body(t, live):
        base = lead + t * unroll
        return q_blocks(base, [full] * unroll, pending=(base - unroll, live))

    live = lax.fori_loop(0, (n_blk - lead) // unroll, loop_body, live)
    finish(n_blk - unroll, unroll, live, stage=0)
    finish(n_blk - unroll, unroll, live, stage=1)


def _sb_attn(qt, k, vt, sgate, tri):
    bsz, n_blk, d_sb, blk = qt.shape
    lp = k.shape[1]
    n_pairs = d_sb // LANES
    return pl.pallas_call(
        functools.partial(_sb_attn_kernel, n_blk=n_blk),
        grid=(bsz, n_pairs),
        in_specs=[
            pl.BlockSpec((None, n_blk, LANES, blk), lambda b, p: (b, 0, p, 0)),
            pl.BlockSpec((None, lp, LANES), lambda b, p: (b, 0, p)),
            pl.BlockSpec((None, n_blk, LANES, blk), lambda b, p: (b, 0, p, 0)),
            pl.BlockSpec((None, lp, LANES), lambda b, p: (b, 0, p)),
            pl.BlockSpec(tri.shape, lambda b, p: (0, 0)),
        ],
        out_specs=pl.BlockSpec((None, lp, LANES), lambda b, p: (b, 0, p)),
        out_shape=jax.ShapeDtypeStruct((bsz, lp, d_sb), jnp.bfloat16),
        scratch_shapes=[pltpu.VMEM((ATT_UNROLL, LANES, blk), jnp.float32),
                        pltpu.VMEM((ATT_UNROLL, 2, 8, blk), jnp.float32)],
        compiler_params=pltpu.CompilerParams(
            dimension_semantics=("parallel", "parallel"), vmem_limit_bytes=VMEM_LIMIT),
        name="sb_attn",
    )(qt, k, vt, sgate, tri)


def _mix_next_kernel(on_ref, cur_ref, halo_ref, cgate_ref, cw_ref, cb_ref, lg_ref, lb_ref, wp_ref,
                     bp_ref, s_ref, h_ref, wo_ref, pg_ref, g_ref, w_ref,
                     o_ref, glu_ref, cgo_ref, k_ref, sgate_ref, qt_ref, vt_ref,
                     xs_ref, y_ref, c_ref, *, n_tiles):
    t = pl.program_id(0)
    conv_tile = jnp.minimum(t, pl.num_programs(0) - 2)
    never = on_ref[0] == 0
    n_chunks = ROW_TILE // CONV_CHUNK
    d_conv = c_ref.shape[-1]

    @pl.when(t == 0)
    def _():
        c_ref[...] = jnp.zeros_like(c_ref)

    c = c_ref[...]
    _conv_window(cur_ref, halo_ref, conv_tile % n_tiles > 0, xs_ref)
    chunks = []

    per_piece = -(-n_chunks // CONV_PIECES)

    def matmul(lhs, rhs):
        p = len(matmul.done)
        waits_for = (p - TIE_BACK + 1) * per_piece - 1
        if 0 <= waits_for < len(chunks):
            tie = jnp.where(never, chunks[waits_for][0:16, 0:LANES], lhs[0:16, 0:LANES])
            lhs = jnp.concatenate(
                [jnp.concatenate([tie, lhs[0:16, LANES:]], axis=1), lhs[16:]], axis=0)
        res = _dot(lhs, rhs)
        matmul.done.append(p)
        for _ in range(per_piece):
            if len(chunks) < n_chunks:
                chunks.append(_conv_chunk(len(chunks), xs_ref, cw_ref, cb_ref, lg_ref, lb_ref,
                                          y_ref, after=(never, res)))
        return res

    matmul.done = []
    mixed = matmul(c, wo_ref[0:d_conv, :]) + matmul(s_ref[...], wo_ref[d_conv:, :])
    ms = jnp.mean(mixed * mixed, axis=-1, keepdims=True)
    h_new = h_ref[...] + mixed * lax.rsqrt(ms + RMS_EPS) * pg_ref[...]
    o_ref[...] = h_new
    _project(h_new, g_ref, w_ref, glu_ref, cgo_ref, k_ref, sgate_ref, qt_ref, vt_ref, matmul=matmul)
    assert len(chunks) == n_chunks
    c_ref[...] = _conv_gate(y_ref, wp_ref, bp_ref, cgate_ref)


def _mix_next(glu, cgate, s, h, conv_w, conv_b, ln_g, ln_b, w_pw2, b_pw2, w_out, post_g,
              pre_g, w_in, layer):
    bsz, lp, dc = glu.shape
    d = h.shape[-1]
    n_tiles = lp // ROW_TILE
    n_sub = ROW_TILE // ATT_BLOCK
    halo_per_tile = ROW_TILE // CONV_HALO
    last = bsz * n_tiles - 1

    def conv_tile(t):
        t = jnp.minimum(t, last)
        return t // n_tiles, t % n_tiles

    def proj_tile(t):
        t = jnp.maximum(t - 1, 0)
        return t // n_tiles, t % n_tiles

    def halo_block(t):
        b, i = conv_tile(t)
        return b, jnp.maximum(i * halo_per_tile - 1, 0), 0

    conv_rows = pl.BlockSpec((None, ROW_TILE, dc), lambda t: conv_tile(t) + (0,))
    proj_rows = lambda width: pl.BlockSpec((None, ROW_TILE, width), lambda t: proj_tile(t) + (0,))
    proj_fm = pl.BlockSpec((None, n_sub, dc, ATT_BLOCK), lambda t: proj_tile(t) + (0, 0))
    vec = lambda width, lyr: pl.BlockSpec((None, 1, width), lambda t: (lyr, 0, 0))
    whole = lambda a, lyr: pl.BlockSpec((None,) + a.shape[1:], lambda t: (lyr, 0, 0))
    row_f32 = jax.ShapeDtypeStruct((bsz, lp, dc), jnp.float32)
    row_bf16 = jax.ShapeDtypeStruct((bsz, lp, dc), jnp.bfloat16)
    fm_bf16 = jax.ShapeDtypeStruct((bsz, lp // ATT_BLOCK, dc, ATT_BLOCK), jnp.bfloat16)
    return pl.pallas_call(
        functools.partial(_mix_next_kernel, n_tiles=n_tiles),
        grid=(bsz * n_tiles + 1,),
        in_specs=[
            pl.BlockSpec(memory_space=pltpu.SMEM),
            conv_rows,
            pl.BlockSpec((None, CONV_HALO, dc), halo_block),
            conv_rows,
            whole(conv_w, layer), vec(dc, layer), vec(dc, layer), vec(dc, layer),
            whole(w_pw2, layer), vec(dc, layer),
            proj_rows(s.shape[-1]), proj_rows(d),
            whole(w_out, layer), vec(d, layer),
            vec(d, layer + 1), whole(w_in, layer + 1),
        ],
        out_specs=[proj_rows(d), proj_rows(dc), proj_rows(dc), proj_rows(dc), proj_rows(dc),
                   proj_fm, proj_fm],
        out_shape=[jax.ShapeDtypeStruct((bsz, lp, d), jnp.float32),
                   row_f32, row_f32, row_bf16, row_f32, fm_bf16, fm_bf16],
        scratch_shapes=[pltpu.VMEM((SUBLANES, ROW_TILE + CONV_HALO, dc), jnp.float32),
                        pltpu.VMEM((ROW_TILE, dc), jnp.bfloat16),
                        pltpu.VMEM((ROW_TILE, dc), jnp.bfloat16)],
        compiler_params=pltpu.CompilerParams(
            dimension_semantics=("arbitrary",),
            vmem_limit_bytes=VMEM_LIMIT_MIX_OUT),
        name="mix_next",
    )(jnp.ones((1,), jnp.int32), glu, glu, cgate, conv_w, conv_b, ln_g, ln_b, w_pw2, b_pw2,
      s, h, w_out, post_g, pre_g, w_in)


def _suffix_sum_matrix():
    r = lax.broadcasted_iota(jnp.int32, (TRI_ROWS, 2 * ATT_BLOCK), 0)
    c = lax.broadcasted_iota(jnp.int32, (TRI_ROWS, 2 * ATT_BLOCK), 1) % ATT_BLOCK
    return jnp.where((r >= ATT_BLOCK) | (c > r), 1.0, 0.0).astype(jnp.bfloat16)


def kernel(x, meta_tokens, pre_norm_g, post_norm_g, w_in, conv_w, conv_b, conv_ln_g, conv_ln_b,
           w_pw2, b_pw2, w_out):
    bsz, seq, d = x.shape
    depth = w_in.shape[0]
    d_conv = conv_w.shape[-1]
    d_sb = N_HEADS * HEAD_DIM
    assert w_in.shape[-1] == 3 * d_conv + 4 * d_sb and d_conv == d_sb
    assert ROW_TILE % ATT_BLOCK == 0 and ROW_TILE % CONV_CHUNK == 0
    assert CONV_HALO % SUBLANES == 0 and CONV_HALO >= CONV_WIDTH - 1 and ROW_TILE % CONV_HALO == 0

    length = N_META + seq
    lp = -(-length // ROW_TILE) * ROW_TILE

    bf16 = jnp.bfloat16
    w_in_b = w_in.astype(bf16)
    w_pw2_b = w_pw2.astype(bf16)
    w_out_b = w_out.astype(bf16)
    vec = lambda a: a.reshape(depth, 1, a.shape[-1])
    pre_g, post_g = vec(pre_norm_g), vec(post_norm_g)
    conv_b3, ln_g3, ln_b3, b_pw3 = vec(conv_b), vec(conv_ln_g), vec(conv_ln_b), vec(b_pw2)
    tri = _suffix_sum_matrix()

    h, glu, cgate, k, sgate, qt, vt = _in_proj_first(
        x, meta_tokens.astype(x.dtype), lp, pre_g, w_in_b, d_conv)
    conv_params = (conv_w, conv_b3, ln_g3, ln_b3, w_pw2_b, b_pw3)
    for layer in range(depth - 1):
        s = _sb_attn(qt, k, vt, sgate, tri)
        h, glu, cgate, k, sgate, qt, vt = _mix_next(
            glu, cgate, s, h, *conv_params, w_out_b, post_g, pre_g, w_in_b, layer)
    s = _sb_attn(qt, k, vt, sgate, tri)
    return _mix_out(glu, cgate, s, h, *conv_params, w_out_b, post_g, depth - 1,
                    final_rows=(N_META, seq))[0]
```

```python
import functools

import jax
import jax.numpy as jnp
from jax import lax
from jax.experimental import pallas as pl
from jax.experimental.pallas import tpu as pltpu

N_META = 16
N_HEADS = 8
HEAD_DIM = 64
CONV_WIDTH = 31
RMS_EPS = 1e-6
LN_EPS = 1e-5
LOG2_E = 1.4426950408889634

LANES = 128
SUBLANES = 8
ROW_TILE = 384
ATT_BLOCK = 128
FIRST_STEP_BLOCKS = 3
OLDEST_ROWS = 32
ATT_UNROLL = 10
CONV_HALO = 32
CONV_CHUNK = 32
TIE_BACK = 2
CONV_PIECES = 6
TRI_ROWS = ATT_BLOCK + 16
LOG_F32_UNDERFLOW = -104.0
VMEM_LIMIT = 48 * 1024 * 1024
VMEM_LIMIT_MIX_OUT = 58 * 1024 * 1024


def _sigmoid(x):
    return 1.0 / (1.0 + jnp.exp2(x * -LOG2_E))


def _dot(a, b):
    return jnp.dot(a, b, preferred_element_type=jnp.float32)


def _project(x, g_ref, w_ref, glu_ref, cgate_ref, k_ref, sgate_ref, qt_ref, vt_ref, matmul=None):
    matmul = matmul or _dot
    d_grp = glu_ref.shape[-1]
    ms = jnp.mean(x * x, axis=-1, keepdims=True)
    u = (x * lax.rsqrt(ms + RMS_EPS) * g_ref[...]).astype(jnp.bfloat16)
    group = lambda m: matmul(u, w_ref[:, m * d_grp:(m + 1) * d_grp])

    glu_ref[...] = group(0) * _sigmoid(group(1))
    cg = group(2)
    cgate_ref[...] = cg * _sigmoid(cg)
    k_ref[...] = group(4).astype(jnp.bfloat16)
    sg = group(6)
    sgate_ref[...] = sg * _sigmoid(sg)

    qt = (group(3) * (HEAD_DIM ** -0.5)).T.astype(jnp.bfloat16)
    vt = group(5).T.astype(jnp.bfloat16)
    for c in range(qt_ref.shape[0]):
        qt_ref[c] = qt[:, c * ATT_BLOCK:(c + 1) * ATT_BLOCK]
        vt_ref[c] = vt[:, c * ATT_BLOCK:(c + 1) * ATT_BLOCK]


def _seq_rows(kind, tile, rows, first_row, n_rows, n_tiles):
    last_rows = first_row + n_rows - (n_tiles - 1) * rows
    if kind == "first":
        return first_row, 0, rows - first_row
    if kind == "last":
        return 0, n_rows - last_rows, last_rows
    return 0, pl.multiple_of(tile * rows - first_row, SUBLANES), rows


def _by_tile_kind(i, n_tiles, fn):
    pl.when(i == 0)(lambda: fn("first"))
    pl.when(jnp.logical_and(i > 0, i < n_tiles - 1))(lambda: fn("mid"))
    pl.when(i == n_tiles - 1)(lambda: fn("last"))


def _in_proj_first_kernel(x_hbm, meta_ref, g_ref, w_ref, h_ref, *rest, n_rows):
    out_refs, (xbuf_ref, sem) = rest[:-2], rest[-2:]
    b, i = pl.program_id(0), pl.program_id(1)
    n_b, n_t = pl.num_programs(0), pl.num_programs(1)
    first_row, rows = meta_ref.shape[0], h_ref.shape[0]
    step = b * n_t + i
    slot = step % 2

    def fetch(kind, batch, tile, to_slot):
        r0, s0, n = _seq_rows(kind, tile, rows, first_row, n_rows, n_t)
        return pltpu.make_async_copy(x_hbm.at[batch, pl.ds(s0, n)],
                                     xbuf_ref.at[to_slot, pl.ds(r0, n)], sem.at[to_slot])

    @pl.when(step == 0)
    def _():
        fetch("first", 0, 0, 0).start()

    @pl.when(step + 1 < n_b * n_t)
    def _():
        nxt = jnp.where(i + 1 < n_t, i + 1, 0)
        nxt_b = jnp.where(i + 1 < n_t, b, b + 1)
        _by_tile_kind(nxt, n_t, lambda kind: fetch(kind, nxt_b, nxt, 1 - slot).start())

    _by_tile_kind(i, n_t, lambda kind: fetch(kind, b, i, slot).wait())

    @pl.when(i == 0)
    def _():
        xbuf_ref[slot, 0:first_row, :] = meta_ref[...]

    valid = _seq_rows("last", i, rows, first_row, n_rows, n_t)[2]
    if valid < rows:
        @pl.when(i == n_t - 1)
        def _():
            xbuf_ref[slot, valid:, :] = jnp.zeros((rows - valid, xbuf_ref.shape[-1]),
                                                  xbuf_ref.dtype)

    x = xbuf_ref[slot]
    h_ref[...] = x
    _project(x, g_ref, w_ref, *out_refs)


def _projection_outputs(bsz, lp, d_grp):
    n_sub = ROW_TILE // ATT_BLOCK
    row_f32 = jax.ShapeDtypeStruct((bsz, lp, d_grp), jnp.float32)
    row_bf16 = jax.ShapeDtypeStruct((bsz, lp, d_grp), jnp.bfloat16)
    fm_bf16 = jax.ShapeDtypeStruct((bsz, lp // ATT_BLOCK, d_grp, ATT_BLOCK), jnp.bfloat16)
    row_spec = pl.BlockSpec((None, ROW_TILE, d_grp), lambda b, i: (b, i, 0))
    fm_spec = pl.BlockSpec((None, n_sub, d_grp, ATT_BLOCK), lambda b, i: (b, i, 0, 0))
    return ([row_f32, row_f32, row_bf16, row_f32, fm_bf16, fm_bf16],
            [row_spec, row_spec, row_spec, row_spec, fm_spec, fm_spec])


def _projection_inputs(pre_g, w_in, layer):
    d = pre_g.shape[-1]
    return ([pre_g, w_in],
            [pl.BlockSpec((None, 1, d), lambda b, i: (layer, 0, 0)),
             pl.BlockSpec((None,) + w_in.shape[1:], lambda b, i: (layer, 0, 0))])


def _in_proj_first(x, meta, lp, pre_g, w_in, d_grp):
    bsz, seq, d = x.shape
    n_tiles = lp // ROW_TILE
    first_row = meta.shape[0]
    assert n_tiles >= 3 and first_row % SUBLANES == 0 and 0 < first_row < ROW_TILE
    assert 0 < first_row + seq - (n_tiles - 1) * ROW_TILE <= ROW_TILE
    out_shape, out_specs = _projection_outputs(bsz, lp, d_grp)
    operands, specs = _projection_inputs(pre_g, w_in, 0)
    h_spec = pl.BlockSpec((None, ROW_TILE, d), lambda b, i: (b, i, 0))
    return pl.pallas_call(
        functools.partial(_in_proj_first_kernel, n_rows=seq),
        grid=(bsz, n_tiles),
        in_specs=[pl.BlockSpec(memory_space=pl.ANY),
                  pl.BlockSpec(meta.shape, lambda b, i: (0, 0))] + specs,
        out_specs=[h_spec] + out_specs,
        out_shape=[jax.ShapeDtypeStruct((bsz, lp, d), x.dtype)] + out_shape,
        scratch_shapes=[pltpu.VMEM((2, ROW_TILE, d), x.dtype), pltpu.SemaphoreType.DMA((2,))],
        compiler_params=pltpu.CompilerParams(
            dimension_semantics=("arbitrary", "arbitrary"),
            vmem_limit_bytes=VMEM_LIMIT),
        name="in_proj",
    )(x, meta, *operands)


def _conv_window(cur_ref, halo_ref, has_left, xs_ref):
    halo = halo_ref[...]
    xs_ref[0, 0:CONV_HALO, :] = jnp.where(has_left, halo, jnp.zeros_like(halo))
    xs_ref[0, CONV_HALO:, :] = cur_ref[...]
    span = ROW_TILE + CONV_HALO - SUBLANES
    for r in range(1, SUBLANES):
        xs_ref[r, 0:span, :] = xs_ref[0, r:r + span, :]


def _conv_chunk(ci, xs_ref, cw_ref, cb_ref, lg_ref, lb_ref, y_ref, after=None):
    r0 = ci * CONV_CHUNK
    first = CONV_HALO - (CONV_WIDTH - 1)
    cols = []
    for lc in range(cb_ref.shape[-1] // LANES):
        lanes = slice(lc * LANES, (lc + 1) * LANES)
        acc = jnp.broadcast_to(cb_ref[:, lanes], (CONV_CHUNK, LANES))
        if after is not None and lc == 0:
            acc = jnp.where(after[0], after[1][0:CONV_CHUNK, 0:LANES], acc)
        for j in range(CONV_WIDTH):
            tiles, r = divmod(first + j, SUBLANES)
            rows = slice(r0 + tiles * SUBLANES, r0 + tiles * SUBLANES + CONV_CHUNK)
            acc = acc + cw_ref[j:j + 1, lanes] * xs_ref[r, rows, lanes]
        cols.append(acc)
    acc = jnp.concatenate(cols, axis=-1)
    mu = jnp.mean(acc, axis=-1, keepdims=True)
    cen = acc - mu
    var = jnp.mean(cen * cen, axis=-1, keepdims=True)
    y = cen * lax.rsqrt(var + LN_EPS) * lg_ref[...] + lb_ref[...]
    y = (y * _sigmoid(y)).astype(jnp.bfloat16)
    y_ref[r0:r0 + CONV_CHUNK, :] = y
    return y


def _conv_gate(y_ref, wp_ref, bp_ref, cgate_ref):
    p = _dot(y_ref[...], wp_ref[...]) + bp_ref[...]
    return (p * cgate_ref[...]).astype(jnp.bfloat16)


def _conv_branch(cur_ref, halo_ref, cgate_ref, cw_ref, cb_ref, lg_ref, lb_ref, wp_ref, bp_ref,
                 xs_ref, y_ref):
    _conv_window(cur_ref, halo_ref, pl.program_id(1) > 0, xs_ref)
    for ci in range(ROW_TILE // CONV_CHUNK):
        _conv_chunk(ci, xs_ref, cw_ref, cb_ref, lg_ref, lb_ref, y_ref)
    return _conv_gate(y_ref, wp_ref, bp_ref, cgate_ref)


def _store_final_rows(h_new, out_hbm, buf_ref, sem, first_row, n_rows):
    b, i = pl.program_id(0), pl.program_id(1)
    n_b, n_t = pl.num_programs(0), pl.num_programs(1)
    rows = buf_ref.shape[0]

    def copy(kind, batch, tile):
        r0, s0, n = _seq_rows(kind, tile, rows, first_row, n_rows, n_t)
        return pltpu.make_async_copy(buf_ref.at[pl.ds(r0, n)], out_hbm.at[batch, pl.ds(s0, n)],
                                     sem.at[0])

    @pl.when(jnp.logical_and(i == 0, b > 0))
    def _():
        copy("last", b - 1, n_t - 1).wait()

    @pl.when(i == 1)
    def _():
        copy("first", b, 0).wait()

    @pl.when(i >= 2)
    def _():
        copy("mid", b, i - 1).wait()

    buf_ref[...] = h_new
    _by_tile_kind(i, n_t, lambda kind: copy(kind, b, i).start())

    @pl.when(jnp.logical_and(i == n_t - 1, b == n_b - 1))
    def _():
        copy("last", b, i).wait()


def _mix_out_kernel(cur_ref, halo_ref, cgate_ref, cw_ref, cb_ref, lg_ref, lb_ref, wp_ref, bp_ref,
                    s_ref, h_ref, wo_ref, pg_ref, out_hbm, xs_ref, y_ref, buf_ref, sem,
                    *, first_row, n_rows):
    c = _conv_branch(cur_ref, halo_ref, cgate_ref, cw_ref, cb_ref, lg_ref, lb_ref, wp_ref, bp_ref,
                     xs_ref, y_ref)
    d_conv = c.shape[-1]
    mixed = _dot(c, wo_ref[0:d_conv, :]) + _dot(s_ref[...], wo_ref[d_conv:, :])
    ms = jnp.mean(mixed * mixed, axis=-1, keepdims=True)
    h_new = h_ref[...] + mixed * lax.rsqrt(ms + RMS_EPS) * pg_ref[...]
    _store_final_rows(h_new, out_hbm, buf_ref, sem, first_row, n_rows)


def _mix_out(glu, cgate, s, h, conv_w, conv_b, ln_g, ln_b, w_pw2, b_pw2, w_out, post_g, layer,
             first_row, n_rows):
    bsz, lp, dc = glu.shape
    d = h.shape[-1]
    halo_per_tile = ROW_TILE // CONV_HALO
    vec_spec = pl.BlockSpec((None, 1, dc), lambda b, i: (layer, 0, 0))
    row_spec = pl.BlockSpec((None, ROW_TILE, dc), lambda b, i: (b, i, 0))
    h_spec = pl.BlockSpec((None, ROW_TILE, d), lambda b, i: (b, i, 0))
    operands = [glu, glu, cgate, conv_w, conv_b, ln_g, ln_b, w_pw2, b_pw2, s, h, w_out, post_g]
    in_specs = [
        row_spec,
        pl.BlockSpec((None, CONV_HALO, dc),
                     lambda b, i: (b, jnp.maximum(i * halo_per_tile - 1, 0), 0)),
        row_spec,
        pl.BlockSpec((None, CONV_WIDTH, dc), lambda b, i: (layer, 0, 0)),
        vec_spec, vec_spec, vec_spec,
        pl.BlockSpec((None, dc, dc), lambda b, i: (layer, 0, 0)),
        vec_spec,
        pl.BlockSpec((None, ROW_TILE, s.shape[-1]), lambda b, i: (b, i, 0)),
        h_spec,
        pl.BlockSpec((None,) + w_out.shape[1:], lambda b, i: (layer, 0, 0)),
        pl.BlockSpec((None, 1, d), lambda b, i: (layer, 0, 0)),
    ]
    n_tiles = lp // ROW_TILE
    assert n_tiles >= 3 and first_row % SUBLANES == 0 and 0 < first_row < ROW_TILE
    assert 0 < first_row + n_rows - (n_tiles - 1) * ROW_TILE <= ROW_TILE
    return pl.pallas_call(
        functools.partial(_mix_out_kernel, first_row=first_row, n_rows=n_rows),
        grid=(bsz, n_tiles),
        in_specs=in_specs,
        out_specs=pl.BlockSpec(memory_space=pl.ANY),
        out_shape=jax.ShapeDtypeStruct((bsz, n_rows, d), jnp.float32),
        scratch_shapes=[pltpu.VMEM((SUBLANES, ROW_TILE + CONV_HALO, dc), jnp.float32),
                        pltpu.VMEM((ROW_TILE, dc), jnp.bfloat16),
                        pltpu.VMEM((ROW_TILE, d), jnp.float32),
                        pltpu.SemaphoreType.DMA((1,))],
        compiler_params=pltpu.CompilerParams(
            dimension_semantics=("arbitrary", "arbitrary"),
            vmem_limit_bytes=VMEM_LIMIT_MIX_OUT),
        name="mix_out",
    )(*operands)


def _log_gates(z):
    log_go = jnp.minimum(z, 0.0) - jnp.log(1.0 + jnp.exp(jnp.minimum(z, -z)))
    return log_go, log_go - z


def _split_bf16(x):
    hi = x.astype(jnp.bfloat16)
    lo = (x - hi.astype(jnp.float32)).astype(jnp.bfloat16)
    return hi, lo


def _add_rows8(x, c8):
    r, n = x.shape
    return (x.reshape(r // 8, 8, n) + c8[None]).reshape(r, n)


def _block_start(j):
    start = j * ATT_BLOCK
    return start if isinstance(start, int) else pl.multiple_of(start, ATT_BLOCK)


def _sb_attn_kernel(qt_ref, k_ref, vt_ref, sgate_ref, tri_ref, s_ref, acc_ref, carry_ref, *, n_blk):
    blk = ATT_BLOCK
    full = FIRST_STEP_BLOCKS
    cut = blk - OLDEST_ROWS
    tri = tri_ref[...]
    row = lax.broadcasted_iota(jnp.int32, (LANES, blk), 0)
    head_rows = (row < HEAD_DIM, row >= HEAD_DIM)
    key_row = lax.broadcasted_iota(jnp.int32, (blk, blk), 0)
    diag_visible = key_row < lax.broadcasted_iota(jnp.int32, (blk, blk), 1)
    head = lambda hh: slice(hh * HEAD_DIM, (hh + 1) * HEAD_DIM)
    lane_blk = lambda m: slice(m * blk, (m + 1) * blk)
    no_rows = jnp.zeros((cut, blk), jnp.bfloat16)

    def load_keys(j):
        return k_ref[pl.ds(_block_start(j), blk), :], vt_ref[j]

    def head_queries(i):
        qt = qt_ref[i]
        zero = jnp.zeros_like(qt)
        return [jnp.where(m, qt, zero) for m in head_rows]

    def write_out(u, i):
        rows = pl.ds(_block_start(i), blk)
        out = acc_ref[u].T
        s_ref[rows, :] = (out * sgate_ref[rows, :]).astype(jnp.bfloat16)

    def older_rows(u, i, j, n_rows):
        q_heads = head_queries(i)
        kb, vb = load_keys(j)
        wanted = key_row < n_rows
        for hh in range(2):
            z = _dot(kb, q_heads[hh])
            log_go, log_stay = _log_gates(z)
            log_stay = jnp.where(wanted, log_stay, 0.0)
            sums = _dot(tri, jnp.concatenate(_split_bf16(log_stay), axis=0))
            carry = carry_ref[u, hh]
            a = jnp.where(wanted, jnp.exp(_add_rows8(log_go + sums[0:blk], carry)), 0.0)
            acc_ref[u, head(hh), :] += _dot(vb[head(hh), :], a.astype(jnp.bfloat16))
            carry_ref[u, hh] = carry + sums[blk:blk + 8]

    def finish(base, n_q, live, stage):
        for u in range(n_q):
            i = base + u
            oldest = i - (full - 1)
            if isinstance(oldest, int) and oldest < 0:
                continue
            if stage == 0:
                @pl.when(jnp.logical_and(live[u] > 0, oldest >= 0))
                def _():
                    def cond(c):
                        j, go = c
                        return jnp.logical_and(j >= 0, go > 0)

                    def body(c):
                        j, _ = c
                        older_rows(u, i, j, jnp.where(j == oldest, cut, blk))
                        go = (jnp.max(carry_ref[u]) >= LOG_F32_UNDERFLOW).astype(jnp.int32)
                        return j - 1, go

                    lax.while_loop(cond, body, (jnp.asarray(oldest, jnp.int32), jnp.int32(1)))
        if stage == 1:
            for u in range(n_q):
                write_out(u, base + u)

    def q_blocks(base, depths, pending=None):
        n_q = len(depths)
        if pending is not None:
            finish(pending[0], n_q, pending[1], stage=0)
            finish(pending[0], n_q, pending[1], stage=1)
        q_heads = [head_queries(base + u) for u in range(n_q)]
        chains = [(u, n, hh) for u in range(n_q) for n in range(depths[u]) for hh in range(2)]
        partial = lambda c: c[1] == full - 1
        offsets = sorted({u - n for u, n, _ in chains})
        keys = {d: load_keys(base + d) for d in offsets}

        z = {}
        for d in offsets:
            users = [c for c in chains if c[0] - c[1] == d]
            wide = _dot(keys[d][0], jnp.concatenate([q_heads[u][hh] for u, _, hh in users], axis=1))
            for m, c in enumerate(users):
                z[c] = wide[cut:, lane_blk(m)] if partial(c) else wide[:, lane_blk(m)]

        log_go, log_stay, halves = {}, {}, []
        for c in chains:
            log_go[c], stay = _log_gates(z[c])
            if c[1] == 0:
                stay = jnp.where(diag_visible, stay, 0.0)
            log_stay[c] = stay
            hi, lo = _split_bf16(stay)
            halves.append(jnp.concatenate([no_rows, hi, no_rows, lo] if partial(c) else [hi, lo],
                                          axis=0))

        wide = _dot(tri, jnp.concatenate(halves, axis=1))
        sums = {c: wide[:, lane_blk(m)] for m, c in enumerate(chains)}

        last = {}
        for u in range(n_q):
            for hh in range(2):
                carry = None
                weights = []
                for n in range(depths[u]):
                    c = (u, n, hh)
                    arg = log_go[c] + (sums[c][cut:blk] if partial(c) else sums[c][0:blk])
                    if carry is not None:
                        arg = _add_rows8(arg, carry)
                    e = jnp.exp(arg)
                    if n == 0:
                        e = jnp.where(diag_visible, e, 0.0)
                    if partial(c):
                        weights.append(no_rows)
                    weights.append(e.astype(jnp.bfloat16))
                    total = sums[c][blk:blk + 8]
                    carry = total if carry is None else carry + total
                values = [keys[u - n][1][head(hh), :] for n in range(depths[u])]
                acc_ref[u, head(hh), :] = _dot(jnp.concatenate(values, axis=1),
                                               jnp.concatenate(weights, axis=0))
                carry_ref[u, hh] = carry
                last[u, hh] = carry

        return tuple(
            (jnp.max(jnp.maximum(last[u, 0], last[u, 1])) >= LOG_F32_UNDERFLOW).astype(jnp.int32)
            for u in range(n_q))

    unroll = ATT_UNROLL
    lead = full - 1
    lead += (n_blk - lead) % unroll
    if lead < unroll:
        lead += unroll
    assert lead <= n_blk and (n_blk - lead) % unroll == 0
    starts = ([0] if lead % unroll else []) + list(range(lead % unroll, lead, unroll))
    for g, base in enumerate(starts):
        end = starts[g + 1] if g + 1 < len(starts) else lead
        live = q_blocks(base, [min(i + 1, full) for i in range(base, end)])
        if end < lead:
            finish(base, end - base, live, stage=0)
            finish(base, end - base, live, stage=1)

    def loop_body(t, live):
        base = lead + t * unroll
        return q_blocks(base, [full] * unroll, pending=(base - unroll, live))

    live = lax.fori_loop(0, (n_blk - lead) // unroll, loop_body, live)
    finish(n_blk - unroll, unroll, live, stage=0)
    finish(n_blk - unroll, unroll, live, stage=1)


def _sb_attn(qt, k, vt, sgate, tri):
    bsz, n_blk, d_sb, blk = qt.shape
    lp = k.shape[1]
    n_pairs = d_sb // LANES
    return pl.pallas_call(
        functools.partial(_sb_attn_kernel, n_blk=n_blk),
        grid=(bsz, n_pairs),
        in_specs=[
            pl.BlockSpec((None, n_blk, LANES, blk), lambda b, p: (b, 0, p, 0)),
            pl.BlockSpec((None, lp, LANES), lambda b, p: (b, 0, p)),
            pl.BlockSpec((None, n_blk, LANES, blk), lambda b, p: (b, 0, p, 0)),
            pl.BlockSpec((None, lp, LANES), lambda b, p: (b, 0, p)),
            pl.BlockSpec(tri.shape, lambda b, p: (0, 0)),
        ],
        out_specs=pl.BlockSpec((None, lp, LANES), lambda b, p: (b, 0, p)),
        out_shape=jax.ShapeDtypeStruct((bsz, lp, d_sb), jnp.bfloat16),
        scratch_shapes=[pltpu.VMEM((ATT_UNROLL, LANES, blk), jnp.float32),
                        pltpu.VMEM((ATT_UNROLL, 2, 8, blk), jnp.float32)],
        compiler_params=pltpu.CompilerParams(
            dimension_semantics=("parallel", "parallel"), vmem_limit_bytes=VMEM_LIMIT),
        name="sb_attn",
    )(qt, k, vt, sgate, tri)


def _mix_next_kernel(on_ref, cur_ref, halo_ref, cgate_ref, cw_ref, cb_ref, lg_ref, lb_ref, wp_ref,
                     bp_ref, s_ref, h_ref, wo_ref, pg_ref, g_ref, w_ref,
                     o_ref, glu_ref, cgo_ref, k_ref, sgate_ref, qt_ref, vt_ref,
                     xs_ref, y_ref, c_ref, *, n_tiles):
    t = pl.program_id(0)
    conv_tile = jnp.minimum(t, pl.num_programs(0) - 2)
    never = on_ref[0] == 0
    n_chunks = ROW_TILE // CONV_CHUNK
    d_conv = c_ref.shape[-1]

    @pl.when(t == 0)
    def _():
        c_ref[...] = jnp.zeros_like(c_ref)

    c = c_ref[...]
    _conv_window(cur_ref, halo_ref, conv_tile % n_tiles > 0, xs_ref)
    chunks = []

    per_piece = -(-n_chunks // CONV_PIECES)

    def matmul(lhs, rhs):
        p = len(matmul.done)
        waits_for = (p - TIE_BACK + 1) * per_piece - 1
        if 0 <= waits_for < len(chunks):
            tie = jnp.where(never, chunks[waits_for][0:16, 0:LANES], lhs[0:16, 0:LANES])
            lhs = jnp.concatenate(
                [jnp.concatenate([tie, lhs[0:16, LANES:]], axis=1), lhs[16:]], axis=0)
        res = _dot(lhs, rhs)
        matmul.done.append(p)
        for _ in range(per_piece):
            if len(chunks) < n_chunks:
                chunks.append(_conv_chunk(len(chunks), xs_ref, cw_ref, cb_ref, lg_ref, lb_ref,
                                          y_ref, after=(never, res)))
        return res

    matmul.done = []
    mixed = matmul(c, wo_ref[0:d_conv, :]) + matmul(s_ref[...], wo_ref[d_conv:, :])
    ms = jnp.mean(mixed * mixed, axis=-1, keepdims=True)
    h_new = h_ref[...] + mixed * lax.rsqrt(ms + RMS_EPS) * pg_ref[...]
    o_ref[...] = h_new
    _project(h_new, g_ref, w_ref, glu_ref, cgo_ref, k_ref, sgate_ref, qt_ref, vt_ref, matmul=matmul)
    assert len(chunks) == n_chunks
    c_ref[...] = _conv_gate(y_ref, wp_ref, bp_ref, cgate_ref)


def _mix_next(glu, cgate, s, h, conv_w, conv_b, ln_g, ln_b, w_pw2, b_pw2, w_out, post_g,
              pre_g, w_in, layer):
    bsz, lp, dc = glu.shape
    d = h.shape[-1]
    n_tiles = lp // ROW_TILE
    n_sub = ROW_TILE // ATT_BLOCK
    halo_per_tile = ROW_TILE // CONV_HALO
    last = bsz * n_tiles - 1

    def conv_tile(t):
        t = jnp.minimum(t, last)
        return t // n_tiles, t % n_tiles

    def proj_tile(t):
        t = jnp.maximum(t - 1, 0)
        return t // n_tiles, t % n_tiles

    def halo_block(t):
        b, i = conv_tile(t)
        return b, jnp.maximum(i * halo_per_tile - 1, 0), 0

    conv_rows = pl.BlockSpec((None, ROW_TILE, dc), lambda t: conv_tile(t) + (0,))
    proj_rows = lambda width: pl.BlockSpec((None, ROW_TILE, width), lambda t: proj_tile(t) + (0,))
    proj_fm = pl.BlockSpec((None, n_sub, dc, ATT_BLOCK), lambda t: proj_tile(t) + (0, 0))
    vec = lambda width, lyr: pl.BlockSpec((None, 1, width), lambda t: (lyr, 0, 0))
    whole = lambda a, lyr: pl.BlockSpec((None,) + a.shape[1:], lambda t: (lyr, 0, 0))
    row_f32 = jax.ShapeDtypeStruct((bsz, lp, dc), jnp.float32)
    row_bf16 = jax.ShapeDtypeStruct((bsz, lp, dc), jnp.bfloat16)
    fm_bf16 = jax.ShapeDtypeStruct((bsz, lp // ATT_BLOCK, dc, ATT_BLOCK), jnp.bfloat16)
    return pl.pallas_call(
        functools.partial(_mix_next_kernel, n_tiles=n_tiles),
        grid=(bsz * n_tiles + 1,),
        in_specs=[
            pl.BlockSpec(memory_space=pltpu.SMEM),
            conv_rows,
            pl.BlockSpec((None, CONV_HALO, dc), halo_block),
            conv_rows,
            whole(conv_w, layer), vec(dc, layer), vec(dc, layer), vec(dc, layer),
            whole(w_pw2, layer), vec(dc, layer),
            proj_rows(s.shape[-1]), proj_rows(d),
            whole(w_out, layer), vec(d, layer),
            vec(d, layer + 1), whole(w_in, layer + 1),
        ],
        out_specs=[proj_rows(d), proj_rows(dc), proj_rows(dc), proj_rows(dc), proj_rows(dc),
                   proj_fm, proj_fm],
        out_shape=[jax.ShapeDtypeStruct((bsz, lp, d), jnp.float32),
                   row_f32, row_f32, row_bf16, row_f32, fm_bf16, fm_bf16],
        scratch_shapes=[pltpu.VMEM((SUBLANES, ROW_TILE + CONV_HALO, dc), jnp.float32),
                        pltpu.VMEM((ROW_TILE, dc), jnp.bfloat16),
                        pltpu.VMEM((ROW_TILE, dc), jnp.bfloat16)],
        compiler_params=pltpu.CompilerParams(
            dimension_semantics=("arbitrary",),
            vmem_limit_bytes=VMEM_LIMIT_MIX_OUT),
        name="mix_next",
    )(jnp.ones((1,), jnp.int32), glu, glu, cgate, conv_w, conv_b, ln_g, ln_b, w_pw2, b_pw2,
      s, h, w_out, post_g, pre_g, w_in)


def _suffix_sum_matrix():
    r = lax.broadcasted_iota(jnp.int32, (TRI_ROWS, 2 * ATT_BLOCK), 0)
    c = lax.broadcasted_iota(jnp.int32, (TRI_ROWS, 2 * ATT_BLOCK), 1) % ATT_BLOCK
    return jnp.where((r >= ATT_BLOCK) | (c > r), 1.0, 0.0).astype(jnp.bfloat16)


def kernel(x, meta_tokens, pre_norm_g, post_norm_g, w_in, conv_w, conv_b, conv_ln_g, conv_ln_b,
           w_pw2, b_pw2, w_out):
    bsz, seq, d = x.shape
    depth = w_in.shape[0]
    d_conv = conv_w.shape[-1]
    d_sb = N_HEADS * HEAD_DIM
    assert w_in.shape[-1] == 3 * d_conv + 4 * d_sb and d_conv == d_sb
    assert ROW_TILE % ATT_BLOCK == 0 and ROW_TILE % CONV_CHUNK == 0
    assert CONV_HALO % SUBLANES == 0 and CONV_HALO >= CONV_WIDTH - 1 and ROW_TILE % CONV_HALO == 0

    length = N_META + seq
    lp = -(-length // ROW_TILE) * ROW_TILE

    bf16 = jnp.bfloat16
    w_in_b = w_in.astype(bf16)
    w_pw2_b = w_pw2.astype(bf16)
    w_out_b = w_out.astype(bf16)
    vec = lambda a: a.reshape(depth, 1, a.shape[-1])
    pre_g, post_g = vec(pre_norm_g), vec(post_norm_g)
    conv_b3, ln_g3, ln_b3, b_pw3 = vec(conv_b), vec(conv_ln_g), vec(conv_ln_b), vec(b_pw2)
    tri = _suffix_sum_matrix()

    h, glu, cgate, k, sgate, qt, vt = _in_proj_first(
        x, meta_tokens.astype(x.dtype), lp, pre_g, w_in_b, d_conv)
    conv_params = (conv_w, conv_b3, ln_g3, ln_b3, w_pw2_b, b_pw3)
    for layer in range(depth - 1):
        s = _sb_attn(qt, k, vt, sgate, tri)
        h, glu, cgate, k, sgate, qt, vt = _mix_next(
            glu, cgate, s, h, *conv_params, w_out_b, post_g, pre_g, w_in_b, layer)
    s = _sb_attn(qt, k, vt, sgate, tri)
    return _mix_out(glu, cgate, s, h, *conv_params, w_out_b, post_g, depth - 1,
                    first_row=N_META, n_rows=seq)
```

```python
import functools

import jax
import jax.numpy as jnp
from jax import lax
from jax.experimental import pallas as pl
from jax.experimental.pallas import tpu as pltpu

N_META = 16
N_HEADS = 8
HEAD_DIM = 64
CONV_WIDTH = 31
RMS_EPS = 1e-6
LN_EPS = 1e-5
LOG2_E = 1.4426950408889634

LANES = 128
SUBLANES = 8
ROW_TILE = 384
ATT_BLOCK = 128
FIRST_STEP_BLOCKS = 3
OLDEST_ROWS = 32
ATT_UNROLL = 10
CONV_HALO = 32
CONV_CHUNK = 32
TIE_BACK = 2
CONV_PIECES = 6
BF16_ROWS = 16
TRI_ROWS = ATT_BLOCK + BF16_ROWS
LOG_F32_UNDERFLOW = -104.0
VMEM_LIMIT = 48 * 1024 * 1024
VMEM_LIMIT_MIX_OUT = 58 * 1024 * 1024


def _sigmoid(x):
    return 1.0 / (1.0 + jnp.exp2(x * -LOG2_E))


def _dot(a, b):
    return jnp.dot(a, b, preferred_element_type=jnp.float32)


def _project(x, g_ref, w_ref, glu_ref, cgate_ref, k_ref, sgate_ref, qt_ref, vt_ref, matmul=None):
    matmul = matmul or _dot
    d_grp = glu_ref.shape[-1]
    ms = jnp.mean(x * x, axis=-1, keepdims=True)
    u = (x * lax.rsqrt(ms + RMS_EPS) * g_ref[...]).astype(jnp.bfloat16)
    group = lambda m: matmul(u, w_ref[:, m * d_grp:(m + 1) * d_grp])

    glu_ref[...] = group(0) * _sigmoid(group(1))
    cg = group(2)
    cgate_ref[...] = cg * _sigmoid(cg)
    k_ref[...] = group(4).astype(jnp.bfloat16)
    sg = group(6)
    sgate_ref[...] = sg * _sigmoid(sg)

    qt = (group(3) * (HEAD_DIM ** -0.5)).T.astype(jnp.bfloat16)
    vt = group(5).T.astype(jnp.bfloat16)
    for c in range(qt_ref.shape[0]):
        qt_ref[c] = qt[:, c * ATT_BLOCK:(c + 1) * ATT_BLOCK]
        vt_ref[c] = vt[:, c * ATT_BLOCK:(c + 1) * ATT_BLOCK]


def _seq_rows(kind, tile, rows, first_row, n_rows, n_tiles):
    last_rows = first_row + n_rows - (n_tiles - 1) * rows
    if kind == "first":
        return first_row, 0, rows - first_row
    if kind == "last":
        return 0, n_rows - last_rows, last_rows
    return 0, pl.multiple_of(tile * rows - first_row, SUBLANES), rows


def _by_tile_kind(i, n_tiles, fn):
    pl.when(i == 0)(lambda: fn("first"))
    pl.when(jnp.logical_and(i > 0, i < n_tiles - 1))(lambda: fn("mid"))
    pl.when(i == n_tiles - 1)(lambda: fn("last"))


def _in_proj_first_kernel(x_hbm, meta_ref, g_ref, w_ref, h_ref, *rest, n_rows):
    out_refs, (xbuf_ref, sem) = rest[:-2], rest[-2:]
    b, i = pl.program_id(0), pl.program_id(1)
    n_b, n_t = pl.num_programs(0), pl.num_programs(1)
    first_row, rows = meta_ref.shape[0], h_ref.shape[0]
    step = b * n_t + i
    slot = step % 2

    def fetch(kind, batch, tile, to_slot):
        r0, s0, n = _seq_rows(kind, tile, rows, first_row, n_rows, n_t)
        return pltpu.make_async_copy(x_hbm.at[batch, pl.ds(s0, n)],
                                     xbuf_ref.at[to_slot, pl.ds(r0, n)], sem.at[to_slot])

    @pl.when(step == 0)
    def _():
        fetch("first", 0, 0, 0).start()

    @pl.when(step + 1 < n_b * n_t)
    def _():
        nxt = jnp.where(i + 1 < n_t, i + 1, 0)
        nxt_b = jnp.where(i + 1 < n_t, b, b + 1)
        _by_tile_kind(nxt, n_t, lambda kind: fetch(kind, nxt_b, nxt, 1 - slot).start())

    _by_tile_kind(i, n_t, lambda kind: fetch(kind, b, i, slot).wait())

    @pl.when(i == 0)
    def _():
        xbuf_ref[slot, 0:first_row, :] = meta_ref[...]

    valid = _seq_rows("last", i, rows, first_row, n_rows, n_t)[2]
    if valid < rows:
        @pl.when(i == n_t - 1)
        def _():
            xbuf_ref[slot, valid:, :] = jnp.zeros((rows - valid, xbuf_ref.shape[-1]),
                                                  xbuf_ref.dtype)

    x = xbuf_ref[slot]
    h_ref[...] = x
    _project(x, g_ref, w_ref, *out_refs)


def _projection_outputs(bsz, lp, d_grp):
    n_sub = ROW_TILE // ATT_BLOCK
    row_f32 = jax.ShapeDtypeStruct((bsz, lp, d_grp), jnp.float32)
    row_bf16 = jax.ShapeDtypeStruct((bsz, lp, d_grp), jnp.bfloat16)
    fm_bf16 = jax.ShapeDtypeStruct((bsz, lp // ATT_BLOCK, d_grp, ATT_BLOCK), jnp.bfloat16)
    row_spec = pl.BlockSpec((None, ROW_TILE, d_grp), lambda b, i: (b, i, 0))
    fm_spec = pl.BlockSpec((None, n_sub, d_grp, ATT_BLOCK), lambda b, i: (b, i, 0, 0))
    return ([row_f32, row_f32, row_bf16, row_f32, fm_bf16, fm_bf16],
            [row_spec, row_spec, row_spec, row_spec, fm_spec, fm_spec])


def _projection_inputs(pre_g, w_in, layer):
    d = pre_g.shape[-1]
    return ([pre_g, w_in],
            [pl.BlockSpec((None, 1, d), lambda b, i: (layer, 0, 0)),
             pl.BlockSpec((None,) + w_in.shape[1:], lambda b, i: (layer, 0, 0))])


def _in_proj_first(x, meta, lp, pre_g, w_in, d_grp):
    bsz, seq, d = x.shape
    n_tiles = lp // ROW_TILE
    first_row = meta.shape[0]
    assert n_tiles >= 3 and first_row % SUBLANES == 0 and 0 < first_row < ROW_TILE
    assert 0 < first_row + seq - (n_tiles - 1) * ROW_TILE <= ROW_TILE
    out_shape, out_specs = _projection_outputs(bsz, lp, d_grp)
    operands, specs = _projection_inputs(pre_g, w_in, 0)
    h_spec = pl.BlockSpec((None, ROW_TILE, d), lambda b, i: (b, i, 0))
    return pl.pallas_call(
        functools.partial(_in_proj_first_kernel, n_rows=seq),
        grid=(bsz, n_tiles),
        in_specs=[pl.BlockSpec(memory_space=pl.ANY),
                  pl.BlockSpec(meta.shape, lambda b, i: (0, 0))] + specs,
        out_specs=[h_spec] + out_specs,
        out_shape=[jax.ShapeDtypeStruct((bsz, lp, d), x.dtype)] + out_shape,
        scratch_shapes=[pltpu.VMEM((2, ROW_TILE, d), x.dtype), pltpu.SemaphoreType.DMA((2,))],
        compiler_params=pltpu.CompilerParams(
            dimension_semantics=("arbitrary", "arbitrary"),
            vmem_limit_bytes=VMEM_LIMIT),
        name="in_proj",
    )(x, meta, *operands)


def _conv_window(cur_ref, halo_ref, has_left, xs_ref):
    halo = halo_ref[...]
    xs_ref[0, 0:CONV_HALO, :] = jnp.where(has_left, halo, jnp.zeros_like(halo))
    xs_ref[0, CONV_HALO:, :] = cur_ref[...]
    span = ROW_TILE + CONV_HALO - SUBLANES
    for r in range(1, SUBLANES):
        xs_ref[r, 0:span, :] = xs_ref[0, r:r + span, :]


def _conv_chunk(ci, xs_ref, cw_ref, cb_ref, lg_ref, lb_ref, y_ref, after=None):
    r0 = ci * CONV_CHUNK
    first = CONV_HALO - (CONV_WIDTH - 1)
    cols = []
    for lc in range(cb_ref.shape[-1] // LANES):
        lanes = slice(lc * LANES, (lc + 1) * LANES)
        acc = jnp.broadcast_to(cb_ref[:, lanes], (CONV_CHUNK, LANES))
        if after is not None and lc == 0:
            acc = jnp.where(after[0], after[1][0:CONV_CHUNK, 0:LANES], acc)
        for j in range(CONV_WIDTH):
            tiles, r = divmod(first + j, SUBLANES)
            rows = slice(r0 + tiles * SUBLANES, r0 + tiles * SUBLANES + CONV_CHUNK)
            acc = acc + cw_ref[j:j + 1, lanes] * xs_ref[r, rows, lanes]
        cols.append(acc)
    acc = jnp.concatenate(cols, axis=-1)
    mu = jnp.mean(acc, axis=-1, keepdims=True)
    cen = acc - mu
    var = jnp.mean(cen * cen, axis=-1, keepdims=True)
    y = cen * lax.rsqrt(var + LN_EPS) * lg_ref[...] + lb_ref[...]
    y = (y * _sigmoid(y)).astype(jnp.bfloat16)
    y_ref[r0:r0 + CONV_CHUNK, :] = y
    return y


def _conv_gate(y_ref, wp_ref, bp_ref, cgate_ref):
    p = _dot(y_ref[...], wp_ref[...]) + bp_ref[...]
    return (p * cgate_ref[...]).astype(jnp.bfloat16)


def _conv_branch(cur_ref, halo_ref, cgate_ref, cw_ref, cb_ref, lg_ref, lb_ref, wp_ref, bp_ref,
                 xs_ref, y_ref):
    _conv_window(cur_ref, halo_ref, pl.program_id(1) > 0, xs_ref)
    for ci in range(ROW_TILE // CONV_CHUNK):
        _conv_chunk(ci, xs_ref, cw_ref, cb_ref, lg_ref, lb_ref, y_ref)
    return _conv_gate(y_ref, wp_ref, bp_ref, cgate_ref)


def _store_final_rows(h_new, out_hbm, buf_ref, sem, first_row, n_rows):
    b, i = pl.program_id(0), pl.program_id(1)
    n_b, n_t = pl.num_programs(0), pl.num_programs(1)
    rows = buf_ref.shape[0]

    def copy(kind, batch, tile):
        r0, s0, n = _seq_rows(kind, tile, rows, first_row, n_rows, n_t)
        return pltpu.make_async_copy(buf_ref.at[pl.ds(r0, n)], out_hbm.at[batch, pl.ds(s0, n)],
                                     sem.at[0])

    @pl.when(jnp.logical_and(i == 0, b > 0))
    def _():
        copy("last", b - 1, n_t - 1).wait()

    @pl.when(i == 1)
    def _():
        copy("first", b, 0).wait()

    @pl.when(i >= 2)
    def _():
        copy("mid", b, i - 1).wait()

    buf_ref[...] = h_new
    _by_tile_kind(i, n_t, lambda kind: copy(kind, b, i).start())

    @pl.when(jnp.logical_and(i == n_t - 1, b == n_b - 1))
    def _():
        copy("last", b, i).wait()


def _mix_out_kernel(cur_ref, halo_ref, cgate_ref, cw_ref, cb_ref, lg_ref, lb_ref, wp_ref, bp_ref,
                    s_ref, h_ref, wo_ref, pg_ref, out_hbm, xs_ref, y_ref, buf_ref, sem,
                    *, first_row, n_rows):
    c = _conv_branch(cur_ref, halo_ref, cgate_ref, cw_ref, cb_ref, lg_ref, lb_ref, wp_ref, bp_ref,
                     xs_ref, y_ref)
    d_conv = c.shape[-1]
    mixed = _dot(c, wo_ref[0:d_conv, :]) + _dot(s_ref[...], wo_ref[d_conv:, :])
    ms = jnp.mean(mixed * mixed, axis=-1, keepdims=True)
    h_new = h_ref[...] + mixed * lax.rsqrt(ms + RMS_EPS) * pg_ref[...]
    _store_final_rows(h_new, out_hbm, buf_ref, sem, first_row, n_rows)


def _mix_out(glu, cgate, s, h, conv_w, conv_b, ln_g, ln_b, w_pw2, b_pw2, w_out, post_g, layer,
             first_row, n_rows):
    bsz, lp, dc = glu.shape
    d = h.shape[-1]
    halo_per_tile = ROW_TILE // CONV_HALO
    vec_spec = pl.BlockSpec((None, 1, dc), lambda b, i: (layer, 0, 0))
    row_spec = pl.BlockSpec((None, ROW_TILE, dc), lambda b, i: (b, i, 0))
    h_spec = pl.BlockSpec((None, ROW_TILE, d), lambda b, i: (b, i, 0))
    operands = [glu, glu, cgate, conv_w, conv_b, ln_g, ln_b, w_pw2, b_pw2, s, h, w_out, post_g]
    in_specs = [
        row_spec,
        pl.BlockSpec((None, CONV_HALO, dc),
                     lambda b, i: (b, jnp.maximum(i * halo_per_tile - 1, 0), 0)),
        row_spec,
        pl.BlockSpec((None, CONV_WIDTH, dc), lambda b, i: (layer, 0, 0)),
        vec_spec, vec_spec, vec_spec,
        pl.BlockSpec((None, dc, dc), lambda b, i: (layer, 0, 0)),
        vec_spec,
        pl.BlockSpec((None, ROW_TILE, s.shape[-1]), lambda b, i: (b, i, 0)),
        h_spec,
        pl.BlockSpec((None,) + w_out.shape[1:], lambda b, i: (layer, 0, 0)),
        pl.BlockSpec((None, 1, d), lambda b, i: (layer, 0, 0)),
    ]
    n_tiles = lp // ROW_TILE
    assert n_tiles >= 3 and first_row % SUBLANES == 0 and 0 < first_row < ROW_TILE
    assert 0 < first_row + n_rows - (n_tiles - 1) * ROW_TILE <= ROW_TILE
    return pl.pallas_call(
        functools.partial(_mix_out_kernel, first_row=first_row, n_rows=n_rows),
        grid=(bsz, n_tiles),
        in_specs=in_specs,
        out_specs=pl.BlockSpec(memory_space=pl.ANY),
        out_shape=jax.ShapeDtypeStruct((bsz, n_rows, d), jnp.float32),
        scratch_shapes=[pltpu.VMEM((SUBLANES, ROW_TILE + CONV_HALO, dc), jnp.float32),
                        pltpu.VMEM((ROW_TILE, dc), jnp.bfloat16),
                        pltpu.VMEM((ROW_TILE, d), jnp.float32),
                        pltpu.SemaphoreType.DMA((1,))],
        compiler_params=pltpu.CompilerParams(
            dimension_semantics=("arbitrary", "arbitrary"),
            vmem_limit_bytes=VMEM_LIMIT_MIX_OUT),
        name="mix_out",
    )(*operands)


def _log_gates(z):
    log_go = jnp.minimum(z, 0.0) - jnp.log(1.0 + jnp.exp(jnp.minimum(z, -z)))
    return log_go, log_go - z


def _split_bf16(x):
    hi = x.astype(jnp.bfloat16)
    lo = (x - hi.astype(jnp.float32)).astype(jnp.bfloat16)
    return hi, lo


def _add_rows8(x, c8):
    r, n = x.shape
    return (x.reshape(r // SUBLANES, SUBLANES, n) + c8[None]).reshape(r, n)


def _block_start(j):
    start = j * ATT_BLOCK
    return start if isinstance(start, int) else pl.multiple_of(start, ATT_BLOCK)


def _sb_attn_kernel(qt_ref, k_ref, vt_ref, sgate_ref, tri_ref, s_ref, acc_ref, carry_ref, *, n_blk):
    blk = ATT_BLOCK
    full = FIRST_STEP_BLOCKS
    cut = blk - OLDEST_ROWS
    tri = tri_ref[...]
    row = lax.broadcasted_iota(jnp.int32, (LANES, blk), 0)
    head_rows = (row < HEAD_DIM, row >= HEAD_DIM)
    key_row = lax.broadcasted_iota(jnp.int32, (blk, blk), 0)
    diag_visible = key_row < lax.broadcasted_iota(jnp.int32, (blk, blk), 1)
    head = lambda hh: slice(hh * HEAD_DIM, (hh + 1) * HEAD_DIM)
    lane_blk = lambda m: slice(m * blk, (m + 1) * blk)
    no_rows = jnp.zeros((cut, blk), jnp.bfloat16)

    def load_keys(j):
        return k_ref[pl.ds(_block_start(j), blk), :], vt_ref[j]

    def head_queries(i):
        qt = qt_ref[i]
        zero = jnp.zeros_like(qt)
        return [jnp.where(m, qt, zero) for m in head_rows]

    def write_out(u, i):
        rows = pl.ds(_block_start(i), blk)
        out = acc_ref[u].T
        s_ref[rows, :] = (out * sgate_ref[rows, :]).astype(jnp.bfloat16)

    def older_rows(u, i, j, n_rows):
        q_heads = head_queries(i)
        kb, vb = load_keys(j)
        wanted = key_row < n_rows
        for hh in range(2):
            z = _dot(kb, q_heads[hh])
            log_go, log_stay = _log_gates(z)
            log_stay = jnp.where(wanted, log_stay, 0.0)
            sums = _dot(tri, jnp.concatenate(_split_bf16(log_stay), axis=0))
            carry = carry_ref[u, hh]
            a = jnp.where(wanted, jnp.exp(_add_rows8(log_go + sums[0:blk], carry)), 0.0)
            acc_ref[u, head(hh), :] += _dot(vb[head(hh), :], a.astype(jnp.bfloat16))
            carry_ref[u, hh] = carry + sums[blk:blk + SUBLANES]

    def finish(base, n_q, live, stage):
        for u in range(n_q):
            i = base + u
            oldest = i - (full - 1)
            if isinstance(oldest, int) and oldest < 0:
                continue
            if stage == 0:
                @pl.when(jnp.logical_and(live[u] > 0, oldest >= 0))
                def _():
                    def cond(c):
                        j, go = c
                        return jnp.logical_and(j >= 0, go > 0)

                    def body(c):
                        j, _ = c
                        older_rows(u, i, j, jnp.where(j == oldest, cut, blk))
                        go = (jnp.max(carry_ref[u]) >= LOG_F32_UNDERFLOW).astype(jnp.int32)
                        return j - 1, go

                    lax.while_loop(cond, body, (jnp.asarray(oldest, jnp.int32), jnp.int32(1)))
        if stage == 1:
            for u in range(n_q):
                write_out(u, base + u)

    def q_blocks(base, depths, pending=None):
        n_q = len(depths)
        if pending is not None:
            finish(pending[0], n_q, pending[1], stage=0)
            finish(pending[0], n_q, pending[1], stage=1)
        q_heads = [head_queries(base + u) for u in range(n_q)]
        chains = [(u, n, hh) for u in range(n_q) for n in range(depths[u]) for hh in range(2)]
        partial = lambda c: c[1] == full - 1
        offsets = sorted({u - n for u, n, _ in chains})
        keys = {d: load_keys(base + d) for d in offsets}

        z = {}
        for d in offsets:
            users = [c for c in chains if c[0] - c[1] == d]
            wide = _dot(keys[d][0], jnp.concatenate([q_heads[u][hh] for u, _, hh in users], axis=1))
            for m, c in enumerate(users):
                z[c] = wide[cut:, lane_blk(m)] if partial(c) else wide[:, lane_blk(m)]

        log_go, log_stay, halves = {}, {}, []
        for c in chains:
            log_go[c], stay = _log_gates(z[c])
            if c[1] == 0:
                stay = jnp.where(diag_visible, stay, 0.0)
            log_stay[c] = stay
            hi, lo = _split_bf16(stay)
            halves.append(jnp.concatenate([no_rows, hi, no_rows, lo] if partial(c) else [hi, lo],
                                          axis=0))

        wide = _dot(tri, jnp.concatenate(halves, axis=1))
        sums = {c: wide[:, lane_blk(m)] for m, c in enumerate(chains)}

        last = {}
        for u in range(n_q):
            for hh in range(2):
                carry = None
                weights = []
                for n in range(depths[u]):
                    c = (u, n, hh)
                    arg = log_go[c] + (sums[c][cut:blk] if partial(c) else sums[c][0:blk])
                    if carry is not None:
                        arg = _add_rows8(arg, carry)
                    e = jnp.exp(arg)
                    if n == 0:
                        e = jnp.where(diag_visible, e, 0.0)
                    if partial(c):
                        weights.append(no_rows)
                    weights.append(e.astype(jnp.bfloat16))
                    total = sums[c][blk:blk + SUBLANES]
                    carry = total if carry is None else carry + total
                values = [keys[u - n][1][head(hh), :] for n in range(depths[u])]
                acc_ref[u, head(hh), :] = _dot(jnp.concatenate(values, axis=1),
                                               jnp.concatenate(weights, axis=0))
                carry_ref[u, hh] = carry
                last[u, hh] = carry

        return tuple(
            (jnp.max(jnp.maximum(last[u, 0], last[u, 1])) >= LOG_F32_UNDERFLOW).astype(jnp.int32)
            for u in range(n_q))

    unroll = ATT_UNROLL
    lead = full - 1
    lead += (n_blk - lead) % unroll
    if lead < unroll:
        lead += unroll
    assert lead <= n_blk and (n_blk - lead) % unroll == 0
    starts = ([0] if lead % unroll else []) + list(range(lead % unroll, lead, unroll))
    for g, base in enumerate(starts):
        end = starts[g + 1] if g + 1 < len(starts) else lead
        live = q_blocks(base, [min(i + 1, full) for i in range(base, end)])
        if end < lead:
            finish(base, end - base, live, stage=0)
            finish(base, end - base, live, stage=1)

    def loop_body(t, live):
        base = lead + t * unroll
        return q_blocks(base, [full] * unroll, pending=(base - unroll, live))

    live = lax.fori_loop(0, (n_blk - lead) // unroll, loop_body, live)
    finish(n_blk - unroll, unroll, live, stage=0)
    finish(n_blk - unroll, unroll, live, stage=1)


def _sb_attn(qt, k, vt, sgate, tri):
    bsz, n_blk, d_sb, blk = qt.shape
    lp = k.shape[1]
    n_pairs = d_sb // LANES
    return pl.pallas_call(
        functools.partial(_sb_attn_kernel, n_blk=n_blk),
        grid=(bsz, n_pairs),
        in_specs=[
            pl.BlockSpec((None, n_blk, LANES, blk), lambda b, p: (b, 0, p, 0)),
            pl.BlockSpec((None, lp, LANES), lambda b, p: (b, 0, p)),
            pl.BlockSpec((None, n_blk, LANES, blk), lambda b, p: (b, 0, p, 0)),
            pl.BlockSpec((None, lp, LANES), lambda b, p: (b, 0, p)),
            pl.BlockSpec(tri.shape, lambda b, p: (0, 0)),
        ],
        out_specs=pl.BlockSpec((None, lp, LANES), lambda b, p: (b, 0, p)),
        out_shape=jax.ShapeDtypeStruct((bsz, lp, d_sb), jnp.bfloat16),
        scratch_shapes=[pltpu.VMEM((ATT_UNROLL, LANES, blk), jnp.float32),
                        pltpu.VMEM((ATT_UNROLL, 2, SUBLANES, blk), jnp.float32)],
        compiler_params=pltpu.CompilerParams(
            dimension_semantics=("parallel", "parallel"), vmem_limit_bytes=VMEM_LIMIT),
        name="sb_attn",
    )(qt, k, vt, sgate, tri)


def _mix_next_kernel(on_ref, cur_ref, halo_ref, cgate_ref, cw_ref, cb_ref, lg_ref, lb_ref, wp_ref,
                     bp_ref, s_ref, h_ref, wo_ref, pg_ref, g_ref, w_ref,
                     o_ref, glu_ref, cgo_ref, k_ref, sgate_ref, qt_ref, vt_ref,
                     xs_ref, y_ref, c_ref, *, n_tiles):
    t = pl.program_id(0)
    conv_tile = jnp.minimum(t, pl.num_programs(0) - 2)
    never = on_ref[0] == 0
    n_chunks = ROW_TILE // CONV_CHUNK
    d_conv = c_ref.shape[-1]

    @pl.when(t == 0)
    def _():
        c_ref[...] = jnp.zeros_like(c_ref)

    c = c_ref[...]
    _conv_window(cur_ref, halo_ref, conv_tile % n_tiles > 0, xs_ref)
    chunks = []

    per_piece = -(-n_chunks // CONV_PIECES)

    def matmul(lhs, rhs):
        p = len(matmul.done)
        waits_for = (p - TIE_BACK + 1) * per_piece - 1
        if 0 <= waits_for < len(chunks):
            top = slice(0, BF16_ROWS)
            tie = jnp.where(never, chunks[waits_for][top, 0:LANES], lhs[top, 0:LANES])
            lhs = jnp.concatenate(
                [jnp.concatenate([tie, lhs[top, LANES:]], axis=1), lhs[BF16_ROWS:]], axis=0)
        res = _dot(lhs, rhs)
        matmul.done.append(p)
        for _ in range(per_piece):
            if len(chunks) < n_chunks:
                chunks.append(_conv_chunk(len(chunks), xs_ref, cw_ref, cb_ref, lg_ref, lb_ref,
                                          y_ref, after=(never, res)))
        return res

    matmul.done = []
    mixed = matmul(c, wo_ref[0:d_conv, :]) + matmul(s_ref[...], wo_ref[d_conv:, :])
    ms = jnp.mean(mixed * mixed, axis=-1, keepdims=True)
    h_new = h_ref[...] + mixed * lax.rsqrt(ms + RMS_EPS) * pg_ref[...]
    o_ref[...] = h_new
    _project(h_new, g_ref, w_ref, glu_ref, cgo_ref, k_ref, sgate_ref, qt_ref, vt_ref, matmul=matmul)
    assert len(chunks) == n_chunks
    c_ref[...] = _conv_gate(y_ref, wp_ref, bp_ref, cgate_ref)


def _mix_next(glu, cgate, s, h, conv_w, conv_b, ln_g, ln_b, w_pw2, b_pw2, w_out, post_g,
              pre_g, w_in, layer):
    bsz, lp, dc = glu.shape
    d = h.shape[-1]
    n_tiles = lp // ROW_TILE
    n_sub = ROW_TILE // ATT_BLOCK
    halo_per_tile = ROW_TILE // CONV_HALO
    last = bsz * n_tiles - 1

    def conv_tile(t):
        t = jnp.minimum(t, last)
        return t // n_tiles, t % n_tiles

    def proj_tile(t):
        t = jnp.maximum(t - 1, 0)
        return t // n_tiles, t % n_tiles

    def halo_block(t):
        b, i = conv_tile(t)
        return b, jnp.maximum(i * halo_per_tile - 1, 0), 0

    conv_rows = pl.BlockSpec((None, ROW_TILE, dc), lambda t: conv_tile(t) + (0,))
    proj_rows = lambda width: pl.BlockSpec((None, ROW_TILE, width), lambda t: proj_tile(t) + (0,))
    proj_fm = pl.BlockSpec((None, n_sub, dc, ATT_BLOCK), lambda t: proj_tile(t) + (0, 0))
    vec = lambda width, lyr: pl.BlockSpec((None, 1, width), lambda t: (lyr, 0, 0))
    whole = lambda a, lyr: pl.BlockSpec((None,) + a.shape[1:], lambda t: (lyr, 0, 0))
    row_f32 = jax.ShapeDtypeStruct((bsz, lp, dc), jnp.float32)
    row_bf16 = jax.ShapeDtypeStruct((bsz, lp, dc), jnp.bfloat16)
    fm_bf16 = jax.ShapeDtypeStruct((bsz, lp // ATT_BLOCK, dc, ATT_BLOCK), jnp.bfloat16)
    return pl.pallas_call(
        functools.partial(_mix_next_kernel, n_tiles=n_tiles),
        grid=(bsz * n_tiles + 1,),
        in_specs=[
            pl.BlockSpec(memory_space=pltpu.SMEM),
            conv_rows,
            pl.BlockSpec((None, CONV_HALO, dc), halo_block),
            conv_rows,
            whole(conv_w, layer), vec(dc, layer), vec(dc, layer), vec(dc, layer),
            whole(w_pw2, layer), vec(dc, layer),
            proj_rows(s.shape[-1]), proj_rows(d),
            whole(w_out, layer), vec(d, layer),
            vec(d, layer + 1), whole(w_in, layer + 1),
        ],
        out_specs=[proj_rows(d), proj_rows(dc), proj_rows(dc), proj_rows(dc), proj_rows(dc),
                   proj_fm, proj_fm],
        out_shape=[jax.ShapeDtypeStruct((bsz, lp, d), jnp.float32),
                   row_f32, row_f32, row_bf16, row_f32, fm_bf16, fm_bf16],
        scratch_shapes=[pltpu.VMEM((SUBLANES, ROW_TILE + CONV_HALO, dc), jnp.float32),
                        pltpu.VMEM((ROW_TILE, dc), jnp.bfloat16),
                        pltpu.VMEM((ROW_TILE, dc), jnp.bfloat16)],
        compiler_params=pltpu.CompilerParams(
            dimension_semantics=("arbitrary",),
            vmem_limit_bytes=VMEM_LIMIT_MIX_OUT),
        name="mix_next",
    )(jnp.ones((1,), jnp.int32), glu, glu, cgate, conv_w, conv_b, ln_g, ln_b, w_pw2, b_pw2,
      s, h, w_out, post_g, pre_g, w_in)


def _suffix_sum_matrix():
    r = lax.broadcasted_iota(jnp.int32, (TRI_ROWS, 2 * ATT_BLOCK), 0)
    c = lax.broadcasted_iota(jnp.int32, (TRI_ROWS, 2 * ATT_BLOCK), 1) % ATT_BLOCK
    return jnp.where((r >= ATT_BLOCK) | (c > r), 1.0, 0.0).astype(jnp.bfloat16)


def kernel(x, meta_tokens, pre_norm_g, post_norm_g, w_in, conv_w, conv_b, conv_ln_g, conv_ln_b,
           w_pw2, b_pw2, w_out):
    bsz, seq, d = x.shape
    depth = w_in.shape[0]
    d_conv = conv_w.shape[-1]
    d_sb = N_HEADS * HEAD_DIM
    assert w_in.shape[-1] == 3 * d_conv + 4 * d_sb and d_conv == d_sb
    assert ROW_TILE % ATT_BLOCK == 0 and ROW_TILE % CONV_CHUNK == 0
    assert CONV_HALO % SUBLANES == 0 and CONV_HALO >= CONV_WIDTH - 1 and ROW_TILE % CONV_HALO == 0

    length = N_META + seq
    lp = -(-length // ROW_TILE) * ROW_TILE

    bf16 = jnp.bfloat16
    w_in_b = w_in.astype(bf16)
    w_pw2_b = w_pw2.astype(bf16)
    w_out_b = w_out.astype(bf16)
    vec = lambda a: a.reshape(depth, 1, a.shape[-1])
    pre_g, post_g = vec(pre_norm_g), vec(post_norm_g)
    conv_b3, ln_g3, ln_b3, b_pw3 = vec(conv_b), vec(conv_ln_g), vec(conv_ln_b), vec(b_pw2)
    tri = _suffix_sum_matrix()

    h, glu, cgate, k, sgate, qt, vt = _in_proj_first(
        x, meta_tokens.astype(x.dtype), lp, pre_g, w_in_b, d_conv)
    conv_params = (conv_w, conv_b3, ln_g3, ln_b3, w_pw2_b, b_pw3)
    for layer in range(depth - 1):
        s = _sb_attn(qt, k, vt, sgate, tri)
        h, glu, cgate, k, sgate, qt, vt = _mix_next(
            glu, cgate, s, h, *conv_params, w_out_b, post_g, pre_g, w_in_b, layer)
    s = _sb_attn(qt, k, vt, sgate, tri)
    return _mix_out(glu, cgate, s, h, *conv_params, w_out_b, post_g, depth - 1,
                    first_row=N_META, n_rows=seq)
```

```python
import functools

import jax
import jax.numpy as jnp
from jax import lax
from jax.experimental import pallas as pl
from jax.experimental.pallas import tpu as pltpu

N_META = 16
N_HEADS = 8
HEAD_DIM = 64
CONV_WIDTH = 31
RMS_EPS = 1e-6
LN_EPS = 1e-5
LOG2_E = 1.4426950408889634

LANES = 128
SUBLANES = 8
ROW_TILE = 384
ATT_BLOCK = 128
FIRST_STEP_BLOCKS = 3
OLDEST_ROWS = 32
ATT_UNROLL = 10
CONV_HALO = 32
CONV_CHUNK = 32
TIE_BACK = 2
CONV_PIECES = 6
BF16_ROWS = 16
TRI_ROWS = ATT_BLOCK + BF16_ROWS
LOG_F32_UNDERFLOW = -104.0
VMEM_LIMIT = 48 * 1024 * 1024
VMEM_LIMIT_MIX_OUT = 58 * 1024 * 1024


def _sigmoid(x):
    return 1.0 / (1.0 + jnp.exp2(x * -LOG2_E))


def _dot(a, b):
    return jnp.dot(a, b, preferred_element_type=jnp.float32)


def _project(x, g_ref, w_ref, glu_ref, cgate_ref, k_ref, sgate_ref, qt_ref, vt_ref, matmul=None):
    matmul = matmul or _dot
    d_grp = glu_ref.shape[-1]
    ms = jnp.mean(x * x, axis=-1, keepdims=True)
    u = (x * lax.rsqrt(ms + RMS_EPS) * g_ref[...]).astype(jnp.bfloat16)
    group = lambda m: matmul(u, w_ref[:, m * d_grp:(m + 1) * d_grp])

    glu_ref[...] = group(0) * _sigmoid(group(1))
    cg = group(2)
    cgate_ref[...] = (cg * _sigmoid(cg)).astype(cgate_ref.dtype)
    k_ref[...] = group(4).astype(jnp.bfloat16)
    sg = group(6)
    sgate_ref[...] = (sg * _sigmoid(sg)).astype(sgate_ref.dtype)

    qt = (group(3) * (HEAD_DIM ** -0.5)).T.astype(jnp.bfloat16)
    vt = group(5).T.astype(jnp.bfloat16)
    for c in range(qt_ref.shape[0]):
        qt_ref[c] = qt[:, c * ATT_BLOCK:(c + 1) * ATT_BLOCK]
        vt_ref[c] = vt[:, c * ATT_BLOCK:(c + 1) * ATT_BLOCK]


def _seq_rows(kind, tile, rows, first_row, n_rows, n_tiles):
    last_rows = first_row + n_rows - (n_tiles - 1) * rows
    if kind == "first":
        return first_row, 0, rows - first_row
    if kind == "last":
        return 0, n_rows - last_rows, last_rows
    return 0, pl.multiple_of(tile * rows - first_row, SUBLANES), rows


def _by_tile_kind(i, n_tiles, fn):
    pl.when(i == 0)(lambda: fn("first"))
    pl.when(jnp.logical_and(i > 0, i < n_tiles - 1))(lambda: fn("mid"))
    pl.when(i == n_tiles - 1)(lambda: fn("last"))


def _in_proj_first_kernel(x_hbm, meta_ref, g_ref, w_ref, h_ref, *rest, n_rows):
    out_refs, (xbuf_ref, sem) = rest[:-2], rest[-2:]
    b, i = pl.program_id(0), pl.program_id(1)
    n_b, n_t = pl.num_programs(0), pl.num_programs(1)
    first_row, rows = meta_ref.shape[0], h_ref.shape[0]
    step = b * n_t + i
    slot = step % 2

    def fetch(kind, batch, tile, to_slot):
        r0, s0, n = _seq_rows(kind, tile, rows, first_row, n_rows, n_t)
        return pltpu.make_async_copy(x_hbm.at[batch, pl.ds(s0, n)],
                                     xbuf_ref.at[to_slot, pl.ds(r0, n)], sem.at[to_slot])

    @pl.when(step == 0)
    def _():
        fetch("first", 0, 0, 0).start()

    @pl.when(step + 1 < n_b * n_t)
    def _():
        nxt = jnp.where(i + 1 < n_t, i + 1, 0)
        nxt_b = jnp.where(i + 1 < n_t, b, b + 1)
        _by_tile_kind(nxt, n_t, lambda kind: fetch(kind, nxt_b, nxt, 1 - slot).start())

    _by_tile_kind(i, n_t, lambda kind: fetch(kind, b, i, slot).wait())

    @pl.when(i == 0)
    def _():
        xbuf_ref[slot, 0:first_row, :] = meta_ref[...]

    valid = _seq_rows("last", i, rows, first_row, n_rows, n_t)[2]
    if valid < rows:
        @pl.when(i == n_t - 1)
        def _():
            xbuf_ref[slot, valid:, :] = jnp.zeros((rows - valid, xbuf_ref.shape[-1]),
                                                  xbuf_ref.dtype)

    x = xbuf_ref[slot]
    h_ref[...] = x
    _project(x, g_ref, w_ref, *out_refs)


def _projection_outputs(bsz, lp, d_grp):
    n_sub = ROW_TILE // ATT_BLOCK
    row_f32 = jax.ShapeDtypeStruct((bsz, lp, d_grp), jnp.float32)
    row_bf16 = jax.ShapeDtypeStruct((bsz, lp, d_grp), jnp.bfloat16)
    fm_bf16 = jax.ShapeDtypeStruct((bsz, lp // ATT_BLOCK, d_grp, ATT_BLOCK), jnp.bfloat16)
    row_spec = pl.BlockSpec((None, ROW_TILE, d_grp), lambda b, i: (b, i, 0))
    fm_spec = pl.BlockSpec((None, n_sub, d_grp, ATT_BLOCK), lambda b, i: (b, i, 0, 0))
    return ([row_f32, row_bf16, row_bf16, row_bf16, fm_bf16, fm_bf16],
            [row_spec, row_spec, row_spec, row_spec, fm_spec, fm_spec])


def _projection_inputs(pre_g, w_in, layer):
    d = pre_g.shape[-1]
    return ([pre_g, w_in],
            [pl.BlockSpec((None, 1, d), lambda b, i: (layer, 0, 0)),
             pl.BlockSpec((None,) + w_in.shape[1:], lambda b, i: (layer, 0, 0))])


def _in_proj_first(x, meta, lp, pre_g, w_in, d_grp):
    bsz, seq, d = x.shape
    n_tiles = lp // ROW_TILE
    first_row = meta.shape[0]
    assert n_tiles >= 3 and first_row % SUBLANES == 0 and 0 < first_row < ROW_TILE
    assert 0 < first_row + seq - (n_tiles - 1) * ROW_TILE <= ROW_TILE
    out_shape, out_specs = _projection_outputs(bsz, lp, d_grp)
    operands, specs = _projection_inputs(pre_g, w_in, 0)
    h_spec = pl.BlockSpec((None, ROW_TILE, d), lambda b, i: (b, i, 0))
    return pl.pallas_call(
        functools.partial(_in_proj_first_kernel, n_rows=seq),
        grid=(bsz, n_tiles),
        in_specs=[pl.BlockSpec(memory_space=pl.ANY),
                  pl.BlockSpec(meta.shape, lambda b, i: (0, 0))] + specs,
        out_specs=[h_spec] + out_specs,
        out_shape=[jax.ShapeDtypeStruct((bsz, lp, d), x.dtype)] + out_shape,
        scratch_shapes=[pltpu.VMEM((2, ROW_TILE, d), x.dtype), pltpu.SemaphoreType.DMA((2,))],
        compiler_params=pltpu.CompilerParams(
            dimension_semantics=("arbitrary", "arbitrary"),
            vmem_limit_bytes=VMEM_LIMIT),
        name="in_proj",
    )(x, meta, *operands)


def _conv_window(cur_ref, halo_ref, has_left, xs_ref):
    halo = halo_ref[...]
    xs_ref[0, 0:CONV_HALO, :] = jnp.where(has_left, halo, jnp.zeros_like(halo))
    xs_ref[0, CONV_HALO:, :] = cur_ref[...]
    span = ROW_TILE + CONV_HALO - SUBLANES
    for r in range(1, SUBLANES):
        xs_ref[r, 0:span, :] = xs_ref[0, r:r + span, :]


def _conv_chunk(ci, xs_ref, cw_ref, cb_ref, lg_ref, lb_ref, y_ref, after=None):
    r0 = ci * CONV_CHUNK
    first = CONV_HALO - (CONV_WIDTH - 1)
    cols = []
    for lc in range(cb_ref.shape[-1] // LANES):
        lanes = slice(lc * LANES, (lc + 1) * LANES)
        acc = jnp.broadcast_to(cb_ref[:, lanes], (CONV_CHUNK, LANES))
        if after is not None and lc == 0:
            acc = jnp.where(after[0], after[1][0:CONV_CHUNK, 0:LANES], acc)
        for j in range(CONV_WIDTH):
            tiles, r = divmod(first + j, SUBLANES)
            rows = slice(r0 + tiles * SUBLANES, r0 + tiles * SUBLANES + CONV_CHUNK)
            acc = acc + cw_ref[j:j + 1, lanes] * xs_ref[r, rows, lanes]
        cols.append(acc)
    acc = jnp.concatenate(cols, axis=-1)
    mu = jnp.mean(acc, axis=-1, keepdims=True)
    cen = acc - mu
    var = jnp.mean(cen * cen, axis=-1, keepdims=True)
    y = cen * lax.rsqrt(var + LN_EPS) * lg_ref[...] + lb_ref[...]
    y = (y * _sigmoid(y)).astype(jnp.bfloat16)
    y_ref[r0:r0 + CONV_CHUNK, :] = y
    return y


def _conv_gate(y_ref, wp_ref, bp_ref, cgate_ref):
    p = _dot(y_ref[...], wp_ref[...]) + bp_ref[...]
    return (p * cgate_ref[...]).astype(jnp.bfloat16)


def _conv_branch(cur_ref, halo_ref, cgate_ref, cw_ref, cb_ref, lg_ref, lb_ref, wp_ref, bp_ref,
                 xs_ref, y_ref):
    _conv_window(cur_ref, halo_ref, pl.program_id(1) > 0, xs_ref)
    for ci in range(ROW_TILE // CONV_CHUNK):
        _conv_chunk(ci, xs_ref, cw_ref, cb_ref, lg_ref, lb_ref, y_ref)
    return _conv_gate(y_ref, wp_ref, bp_ref, cgate_ref)


def _store_final_rows(h_new, out_hbm, buf_ref, sem, first_row, n_rows):
    b, i = pl.program_id(0), pl.program_id(1)
    n_b, n_t = pl.num_programs(0), pl.num_programs(1)
    rows = buf_ref.shape[0]

    def copy(kind, batch, tile):
        r0, s0, n = _seq_rows(kind, tile, rows, first_row, n_rows, n_t)
        return pltpu.make_async_copy(buf_ref.at[pl.ds(r0, n)], out_hbm.at[batch, pl.ds(s0, n)],
                                     sem.at[0])

    @pl.when(jnp.logical_and(i == 0, b > 0))
    def _():
        copy("last", b - 1, n_t - 1).wait()

    @pl.when(i == 1)
    def _():
        copy("first", b, 0).wait()

    @pl.when(i >= 2)
    def _():
        copy("mid", b, i - 1).wait()

    buf_ref[...] = h_new
    _by_tile_kind(i, n_t, lambda kind: copy(kind, b, i).start())

    @pl.when(jnp.logical_and(i == n_t - 1, b == n_b - 1))
    def _():
        copy("last", b, i).wait()


def _mix_out_kernel(cur_ref, halo_ref, cgate_ref, cw_ref, cb_ref, lg_ref, lb_ref, wp_ref, bp_ref,
                    s_ref, h_ref, wo_ref, pg_ref, out_hbm, xs_ref, y_ref, buf_ref, sem,
                    *, first_row, n_rows):
    c = _conv_branch(cur_ref, halo_ref, cgate_ref, cw_ref, cb_ref, lg_ref, lb_ref, wp_ref, bp_ref,
                     xs_ref, y_ref)
    d_conv = c.shape[-1]
    mixed = _dot(c, wo_ref[0:d_conv, :]) + _dot(s_ref[...], wo_ref[d_conv:, :])
    ms = jnp.mean(mixed * mixed, axis=-1, keepdims=True)
    h_new = h_ref[...] + mixed * lax.rsqrt(ms + RMS_EPS) * pg_ref[...]
    _store_final_rows(h_new, out_hbm, buf_ref, sem, first_row, n_rows)


def _mix_out(glu, cgate, s, h, conv_w, conv_b, ln_g, ln_b, w_pw2, b_pw2, w_out, post_g, layer,
             first_row, n_rows):
    bsz, lp, dc = glu.shape
    d = h.shape[-1]
    halo_per_tile = ROW_TILE // CONV_HALO
    vec_spec = pl.BlockSpec((None, 1, dc), lambda b, i: (layer, 0, 0))
    row_spec = pl.BlockSpec((None, ROW_TILE, dc), lambda b, i: (b, i, 0))
    h_spec = pl.BlockSpec((None, ROW_TILE, d), lambda b, i: (b, i, 0))
    operands = [glu, glu, cgate, conv_w, conv_b, ln_g, ln_b, w_pw2, b_pw2, s, h, w_out, post_g]
    in_specs = [
        row_spec,
        pl.BlockSpec((None, CONV_HALO, dc),
                     lambda b, i: (b, jnp.maximum(i * halo_per_tile - 1, 0), 0)),
        row_spec,
        pl.BlockSpec((None, CONV_WIDTH, dc), lambda b, i: (layer, 0, 0)),
        vec_spec, vec_spec, vec_spec,
        pl.BlockSpec((None, dc, dc), lambda b, i: (layer, 0, 0)),
        vec_spec,
        pl.BlockSpec((None, ROW_TILE, s.shape[-1]), lambda b, i: (b, i, 0)),
        h_spec,
        pl.BlockSpec((None,) + w_out.shape[1:], lambda b, i: (layer, 0, 0)),
        pl.BlockSpec((None, 1, d), lambda b, i: (layer, 0, 0)),
    ]
    n_tiles = lp // ROW_TILE
    assert n_tiles >= 3 and first_row % SUBLANES == 0 and 0 < first_row < ROW_TILE
    assert 0 < first_row + n_rows - (n_tiles - 1) * ROW_TILE <= ROW_TILE
    return pl.pallas_call(
        functools.partial(_mix_out_kernel, first_row=first_row, n_rows=n_rows),
        grid=(bsz, n_tiles),
        in_specs=in_specs,
        out_specs=pl.BlockSpec(memory_space=pl.ANY),
        out_shape=jax.ShapeDtypeStruct((bsz, n_rows, d), jnp.float32),
        scratch_shapes=[pltpu.VMEM((SUBLANES, ROW_TILE + CONV_HALO, dc), jnp.float32),
                        pltpu.VMEM((ROW_TILE, dc), jnp.bfloat16),
                        pltpu.VMEM((ROW_TILE, d), jnp.float32),
                        pltpu.SemaphoreType.DMA((1,))],
        compiler_params=pltpu.CompilerParams(
            dimension_semantics=("arbitrary", "arbitrary"),
            vmem_limit_bytes=VMEM_LIMIT_MIX_OUT),
        name="mix_out",
    )(*operands)


def _log_gates(z):
    log_go = jnp.minimum(z, 0.0) - jnp.log(1.0 + jnp.exp(jnp.minimum(z, -z)))
    return log_go, log_go - z


def _split_bf16(x):
    hi = x.astype(jnp.bfloat16)
    lo = (x - hi.astype(jnp.float32)).astype(jnp.bfloat16)
    return hi, lo


def _add_rows8(x, c8):
    r, n = x.shape
    return (x.reshape(r // SUBLANES, SUBLANES, n) + c8[None]).reshape(r, n)


def _block_start(j):
    start = j * ATT_BLOCK
    return start if isinstance(start, int) else pl.multiple_of(start, ATT_BLOCK)


def _sb_attn_kernel(qt_ref, k_ref, vt_ref, sgate_ref, tri_ref, s_ref, acc_ref, carry_ref, *, n_blk):
    blk = ATT_BLOCK
    full = FIRST_STEP_BLOCKS
    cut = blk - OLDEST_ROWS
    tri = tri_ref[...]
    row = lax.broadcasted_iota(jnp.int32, (LANES, blk), 0)
    head_rows = (row < HEAD_DIM, row >= HEAD_DIM)
    key_row = lax.broadcasted_iota(jnp.int32, (blk, blk), 0)
    diag_visible = key_row < lax.broadcasted_iota(jnp.int32, (blk, blk), 1)
    head = lambda hh: slice(hh * HEAD_DIM, (hh + 1) * HEAD_DIM)
    lane_blk = lambda m: slice(m * blk, (m + 1) * blk)
    no_rows = jnp.zeros((cut, blk), jnp.bfloat16)

    def load_keys(j):
        return k_ref[pl.ds(_block_start(j), blk), :], vt_ref[j]

    def head_queries(i):
        qt = qt_ref[i]
        zero = jnp.zeros_like(qt)
        return [jnp.where(m, qt, zero) for m in head_rows]

    def write_out(u, i):
        rows = pl.ds(_block_start(i), blk)
        out = acc_ref[u].T
        s_ref[rows, :] = (out * sgate_ref[rows, :]).astype(jnp.bfloat16)

    def older_rows(u, i, j, n_rows):
        q_heads = head_queries(i)
        kb, vb = load_keys(j)
        wanted = key_row < n_rows
        for hh in range(2):
            z = _dot(kb, q_heads[hh])
            log_go, log_stay = _log_gates(z)
            log_stay = jnp.where(wanted, log_stay, 0.0)
            sums = _dot(tri, jnp.concatenate(_split_bf16(log_stay), axis=0))
            carry = carry_ref[u, hh]
            a = jnp.where(wanted, jnp.exp(_add_rows8(log_go + sums[0:blk], carry)), 0.0)
            acc_ref[u, head(hh), :] += _dot(vb[head(hh), :], a.astype(jnp.bfloat16))
            carry_ref[u, hh] = carry + sums[blk:blk + SUBLANES]

    def finish(base, n_q, live, stage):
        for u in range(n_q):
            i = base + u
            oldest = i - (full - 1)
            if isinstance(oldest, int) and oldest < 0:
                continue
            if stage == 0:
                @pl.when(jnp.logical_and(live[u] > 0, oldest >= 0))
                def _():
                    def cond(c):
                        j, go = c
                        return jnp.logical_and(j >= 0, go > 0)

                    def body(c):
                        j, _ = c
                        older_rows(u, i, j, jnp.where(j == oldest, cut, blk))
                        go = (jnp.max(carry_ref[u]) >= LOG_F32_UNDERFLOW).astype(jnp.int32)
                        return j - 1, go

                    lax.while_loop(cond, body, (jnp.asarray(oldest, jnp.int32), jnp.int32(1)))
        if stage == 1:
            for u in range(n_q):
                write_out(u, base + u)

    def q_blocks(base, depths, pending=None):
        n_q = len(depths)
        if pending is not None:
            finish(pending[0], n_q, pending[1], stage=0)
            finish(pending[0], n_q, pending[1], stage=1)
        q_heads = [head_queries(base + u) for u in range(n_q)]
        chains = [(u, n, hh) for u in range(n_q) for n in range(depths[u]) for hh in range(2)]
        partial = lambda c: c[1] == full - 1
        offsets = sorted({u - n for u, n, _ in chains})
        keys = {d: load_keys(base + d) for d in offsets}

        z = {}
        for d in offsets:
            users = [c for c in chains if c[0] - c[1] == d]
            wide = _dot(keys[d][0], jnp.concatenate([q_heads[u][hh] for u, _, hh in users], axis=1))
            for m, c in enumerate(users):
                z[c] = wide[cut:, lane_blk(m)] if partial(c) else wide[:, lane_blk(m)]

        log_go, log_stay, halves = {}, {}, []
        for c in chains:
            log_go[c], stay = _log_gates(z[c])
            if c[1] == 0:
                stay = jnp.where(diag_visible, stay, 0.0)
            log_stay[c] = stay
            hi, lo = _split_bf16(stay)
            halves.append(jnp.concatenate([no_rows, hi, no_rows, lo] if partial(c) else [hi, lo],
                                          axis=0))

        wide = _dot(tri, jnp.concatenate(halves, axis=1))
        sums = {c: wide[:, lane_blk(m)] for m, c in enumerate(chains)}

        last = {}
        for u in range(n_q):
            for hh in range(2):
                carry = None
                weights = []
                for n in range(depths[u]):
                    c = (u, n, hh)
                    arg = log_go[c] + (sums[c][cut:blk] if partial(c) else sums[c][0:blk])
                    if carry is not None:
                        arg = _add_rows8(arg, carry)
                    e = jnp.exp(arg)
                    if n == 0:
                        e = jnp.where(diag_visible, e, 0.0)
                    if partial(c):
                        weights.append(no_rows)
                    weights.append(e.astype(jnp.bfloat16))
                    total = sums[c][blk:blk + SUBLANES]
                    carry = total if carry is None else carry + total
                values = [keys[u - n][1][head(hh), :] for n in range(depths[u])]
                acc_ref[u, head(hh), :] = _dot(jnp.concatenate(values, axis=1),
                                               jnp.concatenate(weights, axis=0))
                carry_ref[u, hh] = carry
                last[u, hh] = carry

        return tuple(
            (jnp.max(jnp.maximum(last[u, 0], last[u, 1])) >= LOG_F32_UNDERFLOW).astype(jnp.int32)
            for u in range(n_q))

    unroll = ATT_UNROLL
    lead = full - 1
    lead += (n_blk - lead) % unroll
    if lead < unroll:
        lead += unroll
    assert lead <= n_blk and (n_blk - lead) % unroll == 0
    starts = ([0] if lead % unroll else []) + list(range(lead % unroll, lead, unroll))
    for g, base in enumerate(starts):
        end = starts[g + 1] if g + 1 < len(starts) else lead
        live = q_blocks(base, [min(i + 1, full) for i in range(base, end)])
        if end < lead:
            finish(base, end - base, live, stage=0)
            finish(base, end - base, live, stage=1)

    def loop_body(t, live):
        base = lead + t * unroll
        return q_blocks(base, [full] * unroll, pending=(base - unroll, live))

    live = lax.fori_loop(0, (n_blk - lead) // unroll, loop_body, live)
    finish(n_blk - unroll, unroll, live, stage=0)
    finish(n_blk - unroll, unroll, live, stage=1)


def _sb_attn(qt, k, vt, sgate, tri):
    bsz, n_blk, d_sb, blk = qt.shape
    lp = k.shape[1]
    n_pairs = d_sb // LANES
    return pl.pallas_call(
        functools.partial(_sb_attn_kernel, n_blk=n_blk),
        grid=(bsz, n_pairs),
        in_specs=[
            pl.BlockSpec((None, n_blk, LANES, blk), lambda b, p: (b, 0, p, 0)),
            pl.BlockSpec((None, lp, LANES), lambda b, p: (b, 0, p)),
            pl.BlockSpec((None, n_blk, LANES, blk), lambda b, p: (b, 0, p, 0)),
            pl.BlockSpec((None, lp, LANES), lambda b, p: (b, 0, p)),
            pl.BlockSpec(tri.shape, lambda b, p: (0, 0)),
        ],
        out_specs=pl.BlockSpec((None, lp, LANES), lambda b, p: (b, 0, p)),
        out_shape=jax.ShapeDtypeStruct((bsz, lp, d_sb), jnp.bfloat16),
        scratch_shapes=[pltpu.VMEM((ATT_UNROLL, LANES, blk), jnp.float32),
                        pltpu.VMEM((ATT_UNROLL, 2, SUBLANES, blk), jnp.float32)],
        compiler_params=pltpu.CompilerParams(
            dimension_semantics=("parallel", "parallel"), vmem_limit_bytes=VMEM_LIMIT),
        name="sb_attn",
    )(qt, k, vt, sgate, tri)


def _mix_next_kernel(on_ref, cur_ref, halo_ref, cgate_ref, cw_ref, cb_ref, lg_ref, lb_ref, wp_ref,
                     bp_ref, s_ref, h_ref, wo_ref, pg_ref, g_ref, w_ref,
                     o_ref, glu_ref, cgo_ref, k_ref, sgate_ref, qt_ref, vt_ref,
                     xs_ref, y_ref, c_ref, *, n_tiles):
    t = pl.program_id(0)
    conv_tile = jnp.minimum(t, pl.num_programs(0) - 2)
    never = on_ref[0] == 0
    n_chunks = ROW_TILE // CONV_CHUNK
    d_conv = c_ref.shape[-1]

    @pl.when(t == 0)
    def _():
        c_ref[...] = jnp.zeros_like(c_ref)

    c = c_ref[...]
    _conv_window(cur_ref, halo_ref, conv_tile % n_tiles > 0, xs_ref)
    chunks = []

    per_piece = -(-n_chunks // CONV_PIECES)

    def matmul(lhs, rhs):
        p = len(matmul.done)
        waits_for = (p - TIE_BACK + 1) * per_piece - 1
        if 0 <= waits_for < len(chunks):
            top = slice(0, BF16_ROWS)
            tie = jnp.where(never, chunks[waits_for][top, 0:LANES], lhs[top, 0:LANES])
            lhs = jnp.concatenate(
                [jnp.concatenate([tie, lhs[top, LANES:]], axis=1), lhs[BF16_ROWS:]], axis=0)
        res = _dot(lhs, rhs)
        matmul.done.append(p)
        for _ in range(per_piece):
            if len(chunks) < n_chunks:
                chunks.append(_conv_chunk(len(chunks), xs_ref, cw_ref, cb_ref, lg_ref, lb_ref,
                                          y_ref, after=(never, res)))
        return res

    matmul.done = []
    mixed = matmul(c, wo_ref[0:d_conv, :]) + matmul(s_ref[...], wo_ref[d_conv:, :])
    ms = jnp.mean(mixed * mixed, axis=-1, keepdims=True)
    h_new = h_ref[...] + mixed * lax.rsqrt(ms + RMS_EPS) * pg_ref[...]
    o_ref[...] = h_new
    _project(h_new, g_ref, w_ref, glu_ref, cgo_ref, k_ref, sgate_ref, qt_ref, vt_ref, matmul=matmul)
    assert len(chunks) == n_chunks
    c_ref[...] = _conv_gate(y_ref, wp_ref, bp_ref, cgate_ref)


def _mix_next(glu, cgate, s, h, conv_w, conv_b, ln_g, ln_b, w_pw2, b_pw2, w_out, post_g,
              pre_g, w_in, layer):
    bsz, lp, dc = glu.shape
    d = h.shape[-1]
    n_tiles = lp // ROW_TILE
    n_sub = ROW_TILE // ATT_BLOCK
    halo_per_tile = ROW_TILE // CONV_HALO
    last = bsz * n_tiles - 1

    def conv_tile(t):
        t = jnp.minimum(t, last)
        return t // n_tiles, t % n_tiles

    def proj_tile(t):
        t = jnp.maximum(t - 1, 0)
        return t // n_tiles, t % n_tiles

    def halo_block(t):
        b, i = conv_tile(t)
        return b, jnp.maximum(i * halo_per_tile - 1, 0), 0

    conv_rows = pl.BlockSpec((None, ROW_TILE, dc), lambda t: conv_tile(t) + (0,))
    proj_rows = lambda width: pl.BlockSpec((None, ROW_TILE, width), lambda t: proj_tile(t) + (0,))
    proj_fm = pl.BlockSpec((None, n_sub, dc, ATT_BLOCK), lambda t: proj_tile(t) + (0, 0))
    vec = lambda width, lyr: pl.BlockSpec((None, 1, width), lambda t: (lyr, 0, 0))
    whole = lambda a, lyr: pl.BlockSpec((None,) + a.shape[1:], lambda t: (lyr, 0, 0))
    row_f32 = jax.ShapeDtypeStruct((bsz, lp, dc), jnp.float32)
    row_bf16 = jax.ShapeDtypeStruct((bsz, lp, dc), jnp.bfloat16)
    fm_bf16 = jax.ShapeDtypeStruct((bsz, lp // ATT_BLOCK, dc, ATT_BLOCK), jnp.bfloat16)
    return pl.pallas_call(
        functools.partial(_mix_next_kernel, n_tiles=n_tiles),
        grid=(bsz * n_tiles + 1,),
        in_specs=[
            pl.BlockSpec(memory_space=pltpu.SMEM),
            conv_rows,
            pl.BlockSpec((None, CONV_HALO, dc), halo_block),
            conv_rows,
            whole(conv_w, layer), vec(dc, layer), vec(dc, layer), vec(dc, layer),
            whole(w_pw2, layer), vec(dc, layer),
            proj_rows(s.shape[-1]), proj_rows(d),
            whole(w_out, layer), vec(d, layer),
            vec(d, layer + 1), whole(w_in, layer + 1),
        ],
        out_specs=[proj_rows(d), proj_rows(dc), proj_rows(dc), proj_rows(dc), proj_rows(dc),
                   proj_fm, proj_fm],
        out_shape=[jax.ShapeDtypeStruct((bsz, lp, d), jnp.float32),
                   row_f32, row_bf16, row_bf16, row_bf16, fm_bf16, fm_bf16],
        scratch_shapes=[pltpu.VMEM((SUBLANES, ROW_TILE + CONV_HALO, dc), jnp.float32),
                        pltpu.VMEM((ROW_TILE, dc), jnp.bfloat16),
                        pltpu.VMEM((ROW_TILE, dc), jnp.bfloat16)],
        compiler_params=pltpu.CompilerParams(
            dimension_semantics=("arbitrary",),
            vmem_limit_bytes=VMEM_LIMIT_MIX_OUT),
        name="mix_next",
    )(jnp.ones((1,), jnp.int32), glu, glu, cgate, conv_w, conv_b, ln_g, ln_b, w_pw2, b_pw2,
      s, h, w_out, post_g, pre_g, w_in)


def _suffix_sum_matrix():
    r = lax.broadcasted_iota(jnp.int32, (TRI_ROWS, 2 * ATT_BLOCK), 0)
    c = lax.broadcasted_iota(jnp.int32, (TRI_ROWS, 2 * ATT_BLOCK), 1) % ATT_BLOCK
    return jnp.where((r >= ATT_BLOCK) | (c > r), 1.0, 0.0).astype(jnp.bfloat16)


def kernel(x, meta_tokens, pre_norm_g, post_norm_g, w_in, conv_w, conv_b, conv_ln_g, conv_ln_b,
           w_pw2, b_pw2, w_out):
    bsz, seq, d = x.shape
    depth = w_in.shape[0]
    d_conv = conv_w.shape[-1]
    d_sb = N_HEADS * HEAD_DIM
    assert w_in.shape[-1] == 3 * d_conv + 4 * d_sb and d_conv == d_sb
    assert ROW_TILE % ATT_BLOCK == 0 and ROW_TILE % CONV_CHUNK == 0
    assert CONV_HALO % SUBLANES == 0 and CONV_HALO >= CONV_WIDTH - 1 and ROW_TILE % CONV_HALO == 0

    length = N_META + seq
    lp = -(-length // ROW_TILE) * ROW_TILE

    bf16 = jnp.bfloat16
    w_in_b = w_in.astype(bf16)
    w_pw2_b = w_pw2.astype(bf16)
    w_out_b = w_out.astype(bf16)
    vec = lambda a: a.reshape(depth, 1, a.shape[-1])
    pre_g, post_g = vec(pre_norm_g), vec(post_norm_g)
    conv_b3, ln_g3, ln_b3, b_pw3 = vec(conv_b), vec(conv_ln_g), vec(conv_ln_b), vec(b_pw2)
    tri = _suffix_sum_matrix()

    h, glu, cgate, k, sgate, qt, vt = _in_proj_first(
        x, meta_tokens.astype(x.dtype), lp, pre_g, w_in_b, d_conv)
    conv_params = (conv_w, conv_b3, ln_g3, ln_b3, w_pw2_b, b_pw3)
    for layer in range(depth - 1):
        s = _sb_attn(qt, k, vt, sgate, tri)
        h, glu, cgate, k, sgate, qt, vt = _mix_next(
            glu, cgate, s, h, *conv_params, w_out_b, post_g, pre_g, w_in_b, layer)
    s = _sb_attn(qt, k, vt, sgate, tri)
    return _mix_out(glu, cgate, s, h, *conv_params, w_out_b, post_g, depth - 1,
                    first_row=N_META, n_rows=seq)
```

```python
import functools

import jax
import jax.numpy as jnp
from jax import lax
from jax.experimental import pallas as pl
from jax.experimental.pallas import tpu as pltpu

N_META = 16
N_HEADS = 8
HEAD_DIM = 64
CONV_WIDTH = 31
RMS_EPS = 1e-6
LN_EPS = 1e-5
LOG2_E = 1.4426950408889634

LANES = 128
SUBLANES = 8
ROW_TILE = 384
ATT_BLOCK = 128
FIRST_STEP_BLOCKS = 3
OLDEST_ROWS = 32
ATT_UNROLL = 10
CONV_HALO = 32
CONV_CHUNK = 32
TIE_BACK = 2
CONV_PIECES = 6
BF16_ROWS = 16
TRI_ROWS = ATT_BLOCK + BF16_ROWS
LOG_F32_UNDERFLOW = -104.0
VMEM_LIMIT = 48 * 1024 * 1024
VMEM_LIMIT_MIX_OUT = 58 * 1024 * 1024


def _sigmoid(x):
    return 1.0 / (1.0 + jnp.exp2(x * -LOG2_E))


def _dot(a, b):
    return jnp.dot(a, b, preferred_element_type=jnp.float32)


def _project(x, g_ref, w_ref, glu_ref, cgate_ref, k_ref, sgate_ref, qt_ref, vt_ref, matmul=None):
    matmul = matmul or _dot
    d_grp = glu_ref.shape[-1]
    ms = jnp.mean(x * x, axis=-1, keepdims=True)
    u = (x * lax.rsqrt(ms + RMS_EPS) * g_ref[...]).astype(jnp.bfloat16)
    group = lambda m: matmul(u, w_ref[:, m * d_grp:(m + 1) * d_grp])

    glu_ref[...] = group(0) * _sigmoid(group(1))
    cg = group(2)
    cgate_ref[...] = (cg * _sigmoid(cg)).astype(cgate_ref.dtype)
    k_ref[...] = group(4).astype(jnp.bfloat16)
    sg = group(6)
    sgate_ref[...] = (sg * _sigmoid(sg)).astype(sgate_ref.dtype)

    qt = (group(3) * (HEAD_DIM ** -0.5)).T.astype(jnp.bfloat16)
    vt = group(5).T.astype(jnp.bfloat16)
    for c in range(qt_ref.shape[0]):
        qt_ref[c] = qt[:, c * ATT_BLOCK:(c + 1) * ATT_BLOCK]
        vt_ref[c] = vt[:, c * ATT_BLOCK:(c + 1) * ATT_BLOCK]


def _seq_rows(kind, tile, rows, first_row, n_rows, n_tiles):
    last_rows = first_row + n_rows - (n_tiles - 1) * rows
    if kind == "first":
        return first_row, 0, rows - first_row
    if kind == "last":
        return 0, n_rows - last_rows, last_rows
    return 0, pl.multiple_of(tile * rows - first_row, SUBLANES), rows


def _by_tile_kind(i, n_tiles, fn):
    pl.when(i == 0)(lambda: fn("first"))
    pl.when(jnp.logical_and(i > 0, i < n_tiles - 1))(lambda: fn("mid"))
    pl.when(i == n_tiles - 1)(lambda: fn("last"))


def _in_proj_first_kernel(x_hbm, meta_ref, g_ref, w_ref, h_ref, *rest, n_rows):
    out_refs, (xbuf_ref, sem) = rest[:-2], rest[-2:]
    b, i = pl.program_id(0), pl.program_id(1)
    n_b, n_t = pl.num_programs(0), pl.num_programs(1)
    first_row, rows = meta_ref.shape[0], h_ref.shape[0]
    step = b * n_t + i
    slot = step % 2

    def fetch(kind, batch, tile, to_slot):
        r0, s0, n = _seq_rows(kind, tile, rows, first_row, n_rows, n_t)
        return pltpu.make_async_copy(x_hbm.at[batch, pl.ds(s0, n)],
                                     xbuf_ref.at[to_slot, pl.ds(r0, n)], sem.at[to_slot])

    @pl.when(step == 0)
    def _():
        fetch("first", 0, 0, 0).start()

    @pl.when(step + 1 < n_b * n_t)
    def _():
        nxt = jnp.where(i + 1 < n_t, i + 1, 0)
        nxt_b = jnp.where(i + 1 < n_t, b, b + 1)
        _by_tile_kind(nxt, n_t, lambda kind: fetch(kind, nxt_b, nxt, 1 - slot).start())

    _by_tile_kind(i, n_t, lambda kind: fetch(kind, b, i, slot).wait())

    @pl.when(i == 0)
    def _():
        xbuf_ref[slot, 0:first_row, :] = meta_ref[...]

    valid = _seq_rows("last", i, rows, first_row, n_rows, n_t)[2]
    if valid < rows:
        @pl.when(i == n_t - 1)
        def _():
            xbuf_ref[slot, valid:, :] = jnp.zeros((rows - valid, xbuf_ref.shape[-1]),
                                                  xbuf_ref.dtype)

    x = xbuf_ref[slot]
    h_ref[...] = x
    _project(x, g_ref, w_ref, *out_refs)


def _projection_outputs(bsz, lp, d_grp):
    n_sub = ROW_TILE // ATT_BLOCK
    row_f32 = jax.ShapeDtypeStruct((bsz, lp, d_grp), jnp.float32)
    row_bf16 = jax.ShapeDtypeStruct((bsz, lp, d_grp), jnp.bfloat16)
    fm_bf16 = jax.ShapeDtypeStruct((bsz, lp // ATT_BLOCK, d_grp, ATT_BLOCK), jnp.bfloat16)
    row_spec = pl.BlockSpec((None, ROW_TILE, d_grp), lambda b, i: (b, i, 0))
    fm_spec = pl.BlockSpec((None, n_sub, d_grp, ATT_BLOCK), lambda b, i: (b, i, 0, 0))
    return ([row_f32, row_bf16, row_bf16, row_bf16, fm_bf16, fm_bf16],
            [row_spec, row_spec, row_spec, row_spec, fm_spec, fm_spec])


def _projection_inputs(pre_g, w_in, layer):
    d = pre_g.shape[-1]
    return ([pre_g, w_in],
            [pl.BlockSpec((None, 1, d), lambda b, i: (layer, 0, 0)),
             pl.BlockSpec((None,) + w_in.shape[1:], lambda b, i: (layer, 0, 0))])


def _in_proj_first(x, meta, lp, pre_g, w_in, d_grp):
    bsz, seq, d = x.shape
    n_tiles = lp // ROW_TILE
    first_row = meta.shape[0]
    assert n_tiles >= 3 and first_row % SUBLANES == 0 and 0 < first_row < ROW_TILE
    assert 0 < first_row + seq - (n_tiles - 1) * ROW_TILE <= ROW_TILE
    out_shape, out_specs = _projection_outputs(bsz, lp, d_grp)
    operands, specs = _projection_inputs(pre_g, w_in, 0)
    h_spec = pl.BlockSpec((None, ROW_TILE, d), lambda b, i: (b, i, 0))
    return pl.pallas_call(
        functools.partial(_in_proj_first_kernel, n_rows=seq),
        grid=(bsz, n_tiles),
        in_specs=[pl.BlockSpec(memory_space=pl.ANY),
                  pl.BlockSpec(meta.shape, lambda b, i: (0, 0))] + specs,
        out_specs=[h_spec] + out_specs,
        out_shape=[jax.ShapeDtypeStruct((bsz, lp, d), x.dtype)] + out_shape,
        scratch_shapes=[pltpu.VMEM((2, ROW_TILE, d), x.dtype), pltpu.SemaphoreType.DMA((2,))],
        compiler_params=pltpu.CompilerParams(
            dimension_semantics=("arbitrary", "arbitrary"),
            vmem_limit_bytes=VMEM_LIMIT),
        name="in_proj",
    )(x, meta, *operands)


def _conv_window(cur_ref, halo_ref, has_left, xs_ref):
    halo = halo_ref[...]
    xs_ref[0, 0:CONV_HALO, :] = jnp.where(has_left, halo, jnp.zeros_like(halo))
    xs_ref[0, CONV_HALO:, :] = cur_ref[...]
    span = ROW_TILE + CONV_HALO - SUBLANES
    for r in range(1, SUBLANES):
        xs_ref[r, 0:span, :] = xs_ref[0, r:r + span, :]


def _conv_chunk(ci, xs_ref, cw_ref, cb_ref, lg_ref, lb_ref, y_ref, after=None):
    r0 = ci * CONV_CHUNK
    first = CONV_HALO - (CONV_WIDTH - 1)
    cols = []
    for lc in range(cb_ref.shape[-1] // LANES):
        lanes = slice(lc * LANES, (lc + 1) * LANES)
        acc = jnp.broadcast_to(cb_ref[:, lanes], (CONV_CHUNK, LANES))
        if after is not None and lc == 0:
            acc = jnp.where(after[0], after[1][0:CONV_CHUNK, 0:LANES], acc)
        for j in range(CONV_WIDTH):
            tiles, r = divmod(first + j, SUBLANES)
            rows = slice(r0 + tiles * SUBLANES, r0 + tiles * SUBLANES + CONV_CHUNK)
            acc = acc + cw_ref[j:j + 1, lanes] * xs_ref[r, rows, lanes]
        cols.append(acc)
    acc = jnp.concatenate(cols, axis=-1)
    mu = jnp.mean(acc, axis=-1, keepdims=True)
    cen = acc - mu
    var = jnp.mean(cen * cen, axis=-1, keepdims=True)
    y = cen * lax.rsqrt(var + LN_EPS) * lg_ref[...] + lb_ref[...]
    y = (y * _sigmoid(y)).astype(jnp.bfloat16)
    y_ref[r0:r0 + CONV_CHUNK, :] = y
    return y


def _conv_gate(y_ref, wp_ref, bp_ref, cgate_ref):
    p = _dot(y_ref[...], wp_ref[...]) + bp_ref[...]
    return (p * cgate_ref[...]).astype(jnp.bfloat16)


def _conv_branch(cur_ref, halo_ref, cgate_ref, cw_ref, cb_ref, lg_ref, lb_ref, wp_ref, bp_ref,
                 xs_ref, y_ref):
    _conv_window(cur_ref, halo_ref, pl.program_id(1) > 0, xs_ref)
    for ci in range(ROW_TILE // CONV_CHUNK):
        _conv_chunk(ci, xs_ref, cw_ref, cb_ref, lg_ref, lb_ref, y_ref)
    return _conv_gate(y_ref, wp_ref, bp_ref, cgate_ref)


def _store_final_rows(h_new, out_hbm, buf_ref, sem, first_row, n_rows):
    b, i = pl.program_id(0), pl.program_id(1)
    n_b, n_t = pl.num_programs(0), pl.num_programs(1)
    rows = buf_ref.shape[0]

    def copy(kind, batch, tile):
        r0, s0, n = _seq_rows(kind, tile, rows, first_row, n_rows, n_t)
        return pltpu.make_async_copy(buf_ref.at[pl.ds(r0, n)], out_hbm.at[batch, pl.ds(s0, n)],
                                     sem.at[0])

    @pl.when(jnp.logical_and(i == 0, b > 0))
    def _():
        copy("last", b - 1, n_t - 1).wait()

    @pl.when(i == 1)
    def _():
        copy("first", b, 0).wait()

    @pl.when(i >= 2)
    def _():
        copy("mid", b, i - 1).wait()

    buf_ref[...] = h_new
    _by_tile_kind(i, n_t, lambda kind: copy(kind, b, i).start())

    @pl.when(jnp.logical_and(i == n_t - 1, b == n_b - 1))
    def _():
        copy("last", b, i).wait()


def _mix_out_kernel(cur_ref, halo_ref, cgate_ref, cw_ref, cb_ref, lg_ref, lb_ref, wp_ref, bp_ref,
                    s_ref, h_ref, wo_ref, pg_ref, out_hbm, xs_ref, y_ref, buf_ref, sem,
                    *, first_row, n_rows):
    c = _conv_branch(cur_ref, halo_ref, cgate_ref, cw_ref, cb_ref, lg_ref, lb_ref, wp_ref, bp_ref,
                     xs_ref, y_ref)
    d_conv = c.shape[-1]
    mixed = _dot(c, wo_ref[0:d_conv, :]) + _dot(s_ref[...], wo_ref[d_conv:, :])
    ms = jnp.mean(mixed * mixed, axis=-1, keepdims=True)
    h_new = h_ref[...] + mixed * lax.rsqrt(ms + RMS_EPS) * pg_ref[...]
    _store_final_rows(h_new, out_hbm, buf_ref, sem, first_row, n_rows)


def _mix_out(glu, cgate, s, h, conv_w, conv_b, ln_g, ln_b, w_pw2, b_pw2, w_out, post_g, layer,
             first_row, n_rows):
    bsz, lp, dc = glu.shape
    d = h.shape[-1]
    halo_per_tile = ROW_TILE // CONV_HALO
    vec_spec = pl.BlockSpec((None, 1, dc), lambda b, i: (layer, 0, 0))
    row_spec = pl.BlockSpec((None, ROW_TILE, dc), lambda b, i: (b, i, 0))
    h_spec = pl.BlockSpec((None, ROW_TILE, d), lambda b, i: (b, i, 0))
    operands = [glu, glu, cgate, conv_w, conv_b, ln_g, ln_b, w_pw2, b_pw2, s, h, w_out, post_g]
    in_specs = [
        row_spec,
        pl.BlockSpec((None, CONV_HALO, dc),
                     lambda b, i: (b, jnp.maximum(i * halo_per_tile - 1, 0), 0)),
        row_spec,
        pl.BlockSpec((None, CONV_WIDTH, dc), lambda b, i: (layer, 0, 0)),
        vec_spec, vec_spec, vec_spec,
        pl.BlockSpec((None, dc, dc), lambda b, i: (layer, 0, 0)),
        vec_spec,
        pl.BlockSpec((None, ROW_TILE, s.shape[-1]), lambda b, i: (b, i, 0)),
        h_spec,
        pl.BlockSpec((None,) + w_out.shape[1:], lambda b, i: (layer, 0, 0)),
        pl.BlockSpec((None, 1, d), lambda b, i: (layer, 0, 0)),
    ]
    n_tiles = lp // ROW_TILE
    assert n_tiles >= 3 and first_row % SUBLANES == 0 and 0 < first_row < ROW_TILE
    assert 0 < first_row + n_rows - (n_tiles - 1) * ROW_TILE <= ROW_TILE
    return pl.pallas_call(
        functools.partial(_mix_out_kernel, first_row=first_row, n_rows=n_rows),
        grid=(bsz, n_tiles),
        in_specs=in_specs,
        out_specs=pl.BlockSpec(memory_space=pl.ANY),
        out_shape=jax.ShapeDtypeStruct((bsz, n_rows, d), jnp.float32),
        scratch_shapes=[pltpu.VMEM((SUBLANES, ROW_TILE + CONV_HALO, dc), jnp.float32),
                        pltpu.VMEM((ROW_TILE, dc), jnp.bfloat16),
                        pltpu.VMEM((ROW_TILE, d), jnp.float32),
                        pltpu.SemaphoreType.DMA((1,))],
        compiler_params=pltpu.CompilerParams(
            dimension_semantics=("arbitrary", "arbitrary"),
            vmem_limit_bytes=VMEM_LIMIT_MIX_OUT),
        name="mix_out",
    )(*operands)


def _log_gates(z):
    log_go = jnp.minimum(z, 0.0) - jnp.log(1.0 + jnp.exp(jnp.minimum(z, -z)))
    return log_go, log_go - z


def _split_bf16(x):
    hi = x.astype(jnp.bfloat16)
    lo = (x - hi.astype(jnp.float32)).astype(jnp.bfloat16)
    return hi, lo


def _add_rows8(x, c8):
    r, n = x.shape
    return (x.reshape(r // SUBLANES, SUBLANES, n) + c8[None]).reshape(r, n)


def _block_start(j):
    start = j * ATT_BLOCK
    return start if isinstance(start, int) else pl.multiple_of(start, ATT_BLOCK)


def _sb_attn_kernel(qt_ref, k_ref, vt_ref, sgate_ref, tri_ref, s_ref, acc_ref, carry_ref, *, n_blk):
    blk = ATT_BLOCK
    full = FIRST_STEP_BLOCKS
    cut = blk - OLDEST_ROWS
    tri = tri_ref[...]
    row = lax.broadcasted_iota(jnp.int32, (LANES, blk), 0)
    head_rows = (row < HEAD_DIM, row >= HEAD_DIM)
    key_row = lax.broadcasted_iota(jnp.int32, (blk, blk), 0)
    diag_visible = key_row < lax.broadcasted_iota(jnp.int32, (blk, blk), 1)
    head = lambda hh: slice(hh * HEAD_DIM, (hh + 1) * HEAD_DIM)
    lane_blk = lambda m: slice(m * blk, (m + 1) * blk)
    no_rows = jnp.zeros((cut, blk), jnp.bfloat16)

    def load_keys(j):
        return k_ref[pl.ds(_block_start(j), blk), :], vt_ref[j]

    def head_queries(i):
        qt = qt_ref[i]
        zero = jnp.zeros_like(qt)
        return [jnp.where(m, qt, zero) for m in head_rows]

    def write_out(u, i):
        rows = pl.ds(_block_start(i), blk)
        out = acc_ref[u].T
        s_ref[rows, :] = (out * sgate_ref[rows, :]).astype(jnp.bfloat16)

    def older_rows(u, i, j, n_rows):
        q_heads = head_queries(i)
        kb, vb = load_keys(j)
        wanted = key_row < n_rows
        for hh in range(2):
            z = _dot(kb, q_heads[hh])
            log_go, log_stay = _log_gates(z)
            log_stay = jnp.where(wanted, log_stay, 0.0)
            sums = _dot(tri, jnp.concatenate(_split_bf16(log_stay), axis=0))
            carry = carry_ref[u, hh]
            a = jnp.where(wanted, jnp.exp(_add_rows8(log_go + sums[0:blk], carry)), 0.0)
            acc_ref[u, head(hh), :] += _dot(vb[head(hh), :], a.astype(jnp.bfloat16))
            carry_ref[u, hh] = carry + sums[blk:blk + SUBLANES]

    def finish(base, n_q, live, stage):
        for u in range(n_q):
            i = base + u
            oldest = i - (full - 1)
            if isinstance(oldest, int) and oldest < 0:
                continue
            if stage == 0:
                @pl.when(jnp.logical_and(live[u] > 0, oldest >= 0))
                def _():
                    def cond(c):
                        j, go = c
                        return jnp.logical_and(j >= 0, go > 0)

                    def body(c):
                        j, _ = c
                        older_rows(u, i, j, jnp.where(j == oldest, cut, blk))
                        go = (jnp.max(carry_ref[u]) >= LOG_F32_UNDERFLOW).astype(jnp.int32)
                        return j - 1, go

                    lax.while_loop(cond, body, (jnp.asarray(oldest, jnp.int32), jnp.int32(1)))
        if stage == 1:
            for u in range(n_q):
                write_out(u, base + u)

    def q_blocks(base, depths, pending=None):
        n_q = len(depths)
        if pending is not None:
            finish(pending[0], n_q, pending[1], stage=0)
            finish(pending[0], n_q, pending[1], stage=1)
        q_heads = [head_queries(base + u) for u in range(n_q)]
        chains = [(u, n, hh) for u in range(n_q) for n in range(depths[u]) for hh in range(2)]
        partial = lambda c: c[1] == full - 1
        offsets = sorted({u - n for u, n, _ in chains})
        keys = {d: load_keys(base + d) for d in offsets}

        z = {}
        for d in offsets:
            users = [c for c in chains if c[0] - c[1] == d]
            wide = _dot(keys[d][0], jnp.concatenate([q_heads[u][hh] for u, _, hh in users], axis=1))
            for m, c in enumerate(users):
                z[c] = wide[cut:, lane_blk(m)] if partial(c) else wide[:, lane_blk(m)]

        log_go, log_stay, halves = {}, {}, []
        for c in chains:
            log_go[c], stay = _log_gates(z[c])
            if c[1] == 0:
                stay = jnp.where(diag_visible, stay, 0.0)
            log_stay[c] = stay
            hi, lo = _split_bf16(stay)
            halves.append(jnp.concatenate([no_rows, hi, no_rows, lo] if partial(c) else [hi, lo],
                                          axis=0))

        wide = _dot(tri, jnp.concatenate(halves, axis=1))
        sums = {c: wide[:, lane_blk(m)] for m, c in enumerate(chains)}

        last = {}
        for u in range(n_q):
            for hh in range(2):
                carry = None
                weights = []
                for n in range(depths[u]):
                    c = (u, n, hh)
                    arg = log_go[c] + (sums[c][cut:blk] if partial(c) else sums[c][0:blk])
                    if carry is not None:
                        arg = _add_rows8(arg, carry)
                    e = jnp.exp(arg)
                    if n == 0:
                        e = jnp.where(diag_visible, e, 0.0)
                    if partial(c):
                        weights.append(no_rows)
                    weights.append(e.astype(jnp.bfloat16))
                    total = sums[c][blk:blk + SUBLANES]
                    carry = total if carry is None else carry + total
                values = [keys[u - n][1][head(hh), :] for n in range(depths[u])]
                acc_ref[u, head(hh), :] = _dot(jnp.concatenate(values, axis=1),
                                               jnp.concatenate(weights, axis=0))
                carry_ref[u, hh] = carry
                last[u, hh] = carry

        return tuple(
            (jnp.max(jnp.maximum(last[u, 0], last[u, 1])) >= LOG_F32_UNDERFLOW).astype(jnp.int32)
            for u in range(n_q))

    unroll = ATT_UNROLL
    lead = full - 1
    lead += (n_blk - lead) % unroll
    if lead < unroll:
        lead += unroll
    assert lead <= n_blk and (n_blk - lead) % unroll == 0
    starts = ([0] if lead % unroll else []) + list(range(lead % unroll, lead, unroll))
    for g, base in enumerate(starts):
        end = starts[g + 1] if g + 1 < len(starts) else lead
        live = q_blocks(base, [min(i + 1, full) for i in range(base, end)])
        if end < lead:
            finish(base, end - base, live, stage=0)
            finish(base, end - base, live, stage=1)

    def loop_body(t, live):
        base = lead + t * unroll
        return q_blocks(base, [full] * unroll, pending=(base - unroll, live))

    live = lax.fori_loop(0, (n_blk - lead) // unroll, loop_body, live)
    finish(n_blk - unroll, unroll, live, stage=0)
    finish(n_blk - unroll, unroll, live, stage=1)


def _sb_attn(qt, k, vt, sgate, tri):
    bsz, n_blk, d_sb, blk = qt.shape
    lp = k.shape[1]
    n_pairs = d_sb // LANES
    return pl.pallas_call(
        functools.partial(_sb_attn_kernel, n_blk=n_blk),
        grid=(bsz, n_pairs),
        in_specs=[
            pl.BlockSpec((None, n_blk, LANES, blk), lambda b, p: (b, 0, p, 0)),
            pl.BlockSpec((None, lp, LANES), lambda b, p: (b, 0, p)),
            pl.BlockSpec((None, n_blk, LANES, blk), lambda b, p: (b, 0, p, 0)),
            pl.BlockSpec((None, lp, LANES), lambda b, p: (b, 0, p)),
            pl.BlockSpec(tri.shape, lambda b, p: (0, 0)),
        ],
        out_specs=pl.BlockSpec((None, lp, LANES), lambda b, p: (b, 0, p)),
        out_shape=jax.ShapeDtypeStruct((bsz, lp, d_sb), jnp.bfloat16),
        scratch_shapes=[pltpu.VMEM((ATT_UNROLL, LANES, blk), jnp.float32),
                        pltpu.VMEM((ATT_UNROLL, 2, SUBLANES, blk), jnp.float32)],
        compiler_params=pltpu.CompilerParams(
            dimension_semantics=("parallel", "parallel"), vmem_limit_bytes=VMEM_LIMIT),
        name="sb_attn",
    )(qt, k, vt, sgate, tri)


def _mix_next_kernel(on_ref, cur_ref, halo_ref, cgate_ref, cw_ref, cb_ref, lg_ref, lb_ref, wp_ref,
                     bp_ref, s_ref, h_ref, wo_ref, pg_ref, g_ref, w_ref,
                     o_ref, glu_ref, cgo_ref, k_ref, sgate_ref, qt_ref, vt_ref,
                     xs_ref, y_ref, c_ref, *, n_tiles):
    t = pl.program_id(0)
    conv_tile = jnp.minimum(t, pl.num_programs(0) - 2)
    never = on_ref[0] == 0
    n_chunks = ROW_TILE // CONV_CHUNK
    d_conv = c_ref.shape[-1]

    @pl.when(t == 0)
    def _():
        c_ref[...] = jnp.zeros_like(c_ref)

    c = c_ref[...]
    _conv_window(cur_ref, halo_ref, conv_tile % n_tiles > 0, xs_ref)
    chunks = []

    per_piece = -(-n_chunks // CONV_PIECES)

    def matmul(lhs, rhs):
        p = len(matmul.done)
        waits_for = (p - TIE_BACK + 1) * per_piece - 1
        if 0 <= waits_for < len(chunks):
            top = slice(0, BF16_ROWS)
            tie = jnp.where(never, chunks[waits_for][top, 0:LANES], lhs[top, 0:LANES])
            lhs = jnp.concatenate(
                [jnp.concatenate([tie, lhs[top, LANES:]], axis=1), lhs[BF16_ROWS:]], axis=0)
        res = _dot(lhs, rhs)
        matmul.done.append(p)
        for _ in range(per_piece):
            if len(chunks) < n_chunks:
                chunks.append(_conv_chunk(len(chunks), xs_ref, cw_ref, cb_ref, lg_ref, lb_ref,
                                          y_ref, after=(never, res)))
        return res

    matmul.done = []
    mixed = matmul(c, wo_ref[0:d_conv, :]) + matmul(s_ref[...], wo_ref[d_conv:, :])
    ms = jnp.mean(mixed * mixed, axis=-1, keepdims=True)
    h_new = h_ref[...] + mixed * lax.rsqrt(ms + RMS_EPS) * pg_ref[...]
    o_ref[...] = h_new
    _project(h_new, g_ref, w_ref, glu_ref, cgo_ref, k_ref, sgate_ref, qt_ref, vt_ref, matmul=matmul)
    assert len(chunks) == n_chunks
    c_ref[...] = _conv_gate(y_ref, wp_ref, bp_ref, cgate_ref)


def _mix_next(glu, cgate, s, h, conv_w, conv_b, ln_g, ln_b, w_pw2, b_pw2, w_out, post_g,
              pre_g, w_in, layer):
    bsz, lp, dc = glu.shape
    d = h.shape[-1]
    n_tiles = lp // ROW_TILE
    n_sub = ROW_TILE // ATT_BLOCK
    halo_per_tile = ROW_TILE // CONV_HALO
    last = bsz * n_tiles - 1

    def conv_tile(t):
        t = jnp.minimum(t, last)
        return t // n_tiles, t % n_tiles

    def proj_tile(t):
        t = jnp.maximum(t - 1, 0)
        return t // n_tiles, t % n_tiles

    def halo_block(t):
        b, i = conv_tile(t)
        return b, jnp.maximum(i * halo_per_tile - 1, 0), 0

    conv_rows = pl.BlockSpec((None, ROW_TILE, dc), lambda t: conv_tile(t) + (0,))
    proj_rows = lambda width: pl.BlockSpec((None, ROW_TILE, width), lambda t: proj_tile(t) + (0,))
    proj_fm = pl.BlockSpec((None, n_sub, dc, ATT_BLOCK), lambda t: proj_tile(t) + (0, 0))
    vec = lambda width, lyr: pl.BlockSpec((None, 1, width), lambda t: (lyr, 0, 0))
    whole = lambda a, lyr: pl.BlockSpec((None,) + a.shape[1:], lambda t: (lyr, 0, 0),
                                        pipeline_mode=pl.Buffered(1))
    row_f32 = jax.ShapeDtypeStruct((bsz, lp, dc), jnp.float32)
    row_bf16 = jax.ShapeDtypeStruct((bsz, lp, dc), jnp.bfloat16)
    fm_bf16 = jax.ShapeDtypeStruct((bsz, lp // ATT_BLOCK, dc, ATT_BLOCK), jnp.bfloat16)
    return pl.pallas_call(
        functools.partial(_mix_next_kernel, n_tiles=n_tiles),
        grid=(bsz * n_tiles + 1,),
        in_specs=[
            pl.BlockSpec(memory_space=pltpu.SMEM),
            conv_rows,
            pl.BlockSpec((None, CONV_HALO, dc), halo_block),
            conv_rows,
            whole(conv_w, layer), vec(dc, layer), vec(dc, layer), vec(dc, layer),
            whole(w_pw2, layer), vec(dc, layer),
            proj_rows(s.shape[-1]), proj_rows(d),
            whole(w_out, layer), vec(d, layer),
            vec(d, layer + 1), whole(w_in, layer + 1),
        ],
        out_specs=[proj_rows(d), proj_rows(dc), proj_rows(dc), proj_rows(dc), proj_rows(dc),
                   proj_fm, proj_fm],
        out_shape=[jax.ShapeDtypeStruct((bsz, lp, d), jnp.float32),
                   row_f32, row_bf16, row_bf16, row_bf16, fm_bf16, fm_bf16],
        scratch_shapes=[pltpu.VMEM((SUBLANES, ROW_TILE + CONV_HALO, dc), jnp.float32),
                        pltpu.VMEM((ROW_TILE, dc), jnp.bfloat16),
                        pltpu.VMEM((ROW_TILE, dc), jnp.bfloat16)],
        compiler_params=pltpu.CompilerParams(
            dimension_semantics=("arbitrary",),
            vmem_limit_bytes=VMEM_LIMIT_MIX_OUT),
        name="mix_next",
    )(jnp.ones((1,), jnp.int32), glu, glu, cgate, conv_w, conv_b, ln_g, ln_b, w_pw2, b_pw2,
      s, h, w_out, post_g, pre_g, w_in)


def _suffix_sum_matrix():
    r = lax.broadcasted_iota(jnp.int32, (TRI_ROWS, 2 * ATT_BLOCK), 0)
    c = lax.broadcasted_iota(jnp.int32, (TRI_ROWS, 2 * ATT_BLOCK), 1) % ATT_BLOCK
    return jnp.where((r >= ATT_BLOCK) | (c > r), 1.0, 0.0).astype(jnp.bfloat16)


def kernel(x, meta_tokens, pre_norm_g, post_norm_g, w_in, conv_w, conv_b, conv_ln_g, conv_ln_b,
           w_pw2, b_pw2, w_out):
    bsz, seq, d = x.shape
    depth = w_in.shape[0]
    d_conv = conv_w.shape[-1]
    d_sb = N_HEADS * HEAD_DIM
    assert w_in.shape[-1] == 3 * d_conv + 4 * d_sb and d_conv == d_sb
    assert ROW_TILE % ATT_BLOCK == 0 and ROW_TILE % CONV_CHUNK == 0
    assert CONV_HALO % SUBLANES == 0 and CONV_HALO >= CONV_WIDTH - 1 and ROW_TILE % CONV_HALO == 0

    length = N_META + seq
    lp = -(-length // ROW_TILE) * ROW_TILE

    bf16 = jnp.bfloat16
    w_in_b = w_in.astype(bf16)
    w_pw2_b = w_pw2.astype(bf16)
    w_out_b = w_out.astype(bf16)
    vec = lambda a: a.reshape(depth, 1, a.shape[-1])
    pre_g, post_g = vec(pre_norm_g), vec(post_norm_g)
    conv_b3, ln_g3, ln_b3, b_pw3 = vec(conv_b), vec(conv_ln_g), vec(conv_ln_b), vec(b_pw2)
    tri = _suffix_sum_matrix()

    h, glu, cgate, k, sgate, qt, vt = _in_proj_first(
        x, meta_tokens.astype(x.dtype), lp, pre_g, w_in_b, d_conv)
    conv_params = (conv_w, conv_b3, ln_g3, ln_b3, w_pw2_b, b_pw3)
    for layer in range(depth - 1):
        s = _sb_attn(qt, k, vt, sgate, tri)
        h, glu, cgate, k, sgate, qt, vt = _mix_next(
            glu, cgate, s, h, *conv_params, w_out_b, post_g, pre_g, w_in_b, layer)
    s = _sb_attn(qt, k, vt, sgate, tri)
    return _mix_out(glu, cgate, s, h, *conv_params, w_out_b, post_g, depth - 1,
                    first_row=N_META, n_rows=seq)
```
